```python
import math
import jax
import jax.numpy as jnp
from jax import lax
import numpy as np

D_MODEL = 2048
BATCH = 2
SEQ = 16384
DEPTH = 4

PLE_DIM = 256
HEAD_DIM = 64
A_WIDTH = D_MODEL // 2
A_HEADS = A_WIDTH // HEAD_DIM
DECAY_LORA = max(32, int(round(1.8 * A_WIDTH ** 0.5 / 32)) * 32)
AAA_LORA = max(32, int(round(1.8 * A_WIDTH ** 0.5 / 32)) * 32)
GATE_LORA = max(32, int(round(0.6 * A_WIDTH ** 0.8 / 32)) * 32)
A_COLS = 3 * A_WIDTH + DECAY_LORA + AAA_LORA + GATE_LORA
WKV_GN_EPS = 64e-5
B_WIDTH = D_MODEL // 2
B_Q_HEADS = B_WIDTH // HEAD_DIM
B_KV_HEADS = max(1, B_Q_HEADS // 8)
B_GROUP = B_Q_HEADS // B_KV_HEADS
B_KV_WIDTH = B_KV_HEADS * HEAD_DIM
B_COLS = B_WIDTH + 2 * B_KV_WIDTH
WINDOW = 128
BLOCK = 128
EVEN_COLS = A_COLS + B_COLS
EVEN_MIX = A_WIDTH + B_WIDTH
DIFF_HEAD_DIM = 128
DIFF_HEADS = D_MODEL // (2 * DIFF_HEAD_DIM)
DIFF_WIDTH = DIFF_HEADS * 2 * DIFF_HEAD_DIM
ODD_COLS = 3 * DIFF_WIDTH
ROPE_THETA = 500000.0
ROPE_FRACTION = 4
FFN_DIM = -(-8 * D_MODEL // (3 * 256)) * 256
LN_EPS = 1e-5

kernel_name = 'hybrid_rwkv7_swa_sink_diffattn_deepnorm'


def layer_norm(x, g, b):
    xf = x.astype(jnp.float32)
    mu = jnp.mean(xf, -1, keepdims=True)
    var = jnp.mean(jnp.square(xf - mu), -1, keepdims=True)
    y = (xf - mu) * lax.rsqrt(var + LN_EPS) * g.astype(jnp.float32) + b.astype(jnp.float32)
    return y.astype(x.dtype)


def partial_rotary(t, positions):
    hd = t.shape[-1]
    rd = hd // ROPE_FRACTION
    half = rd // 2
    inv_freq = ROPE_THETA ** (-jnp.arange(half, dtype=jnp.float32) / half)
    ang = positions.astype(jnp.float32)[:, :, None] * inv_freq
    cos = jnp.cos(ang)[:, :, None, :]
    sin = jnp.sin(ang)[:, :, None, :]
    tr = t[..., :rd].astype(jnp.float32)
    t1, t2 = tr[..., :half], tr[..., half:]
    rot = jnp.concatenate([t1 * cos - t2 * sin, t2 * cos + t1 * sin], -1).astype(t.dtype)
    return jnp.concatenate([rot, t[..., rd:]], -1)


def token_shift(h):
    return jnp.pad(h, ((0, 0), (1, 0), (0, 0)))[:, :-1]


def wkv7_step(state, inp):
    r_t, w_t, k_t, v_t, a_t, b_t = inp
    sa = jnp.einsum('bhij,bhj->bhi', state, a_t)
    state = (state * w_t[:, :, None, :] + sa[..., None] * b_t[:, :, None, :]
             + v_t[..., None] * k_t[:, :, None, :])
    return state, jnp.einsum('bhij,bhj->bhi', state, r_t)


def rwkv7_mix(h, mu, w0, w_up, a0, a_up, g_up, k_k, k_a, r_k, gn_w, gn_b):
    bsz, seq = h.shape[0], h.shape[1]
    f32 = jnp.float32
    h = h + (token_shift(h) - h) * mu
    r, k, v, lw, la, lg = jnp.split(
        h, [A_WIDTH, 2 * A_WIDTH, 3 * A_WIDTH, 3 * A_WIDTH + DECAY_LORA,
            3 * A_WIDTH + DECAY_LORA + AAA_LORA], axis=-1)
    w = -jax.nn.softplus(-(w0 + jnp.tanh(lw) @ w_up)) - 0.5
    decay = jnp.exp(-jnp.exp(w.astype(f32)))
    a = jax.nn.sigmoid(a0 + la @ a_up)
    g = jax.nn.sigmoid(lg) @ g_up
    heads = lambda t: t.astype(f32).reshape(bsz, seq, A_HEADS, HEAD_DIM)
    kk = heads(k * k_k)
    kk = kk / jnp.maximum(jnp.sqrt(jnp.sum(kk * kk, -1, keepdims=True)), 1e-12)
    k = k * (1.0 + (a - 1.0) * k_a)
    r_h, k_h, v_h, a_h, w_h = heads(r), heads(k), heads(v), heads(a), heads(decay)
    xs = tuple(jnp.moveaxis(t, 1, 0) for t in (r_h, w_h, k_h, v_h, -kk, kk * a_h))
    state0 = jnp.zeros((bsz, A_HEADS, HEAD_DIM, HEAD_DIM), f32)
    _, y = lax.scan(wkv7_step, state0, xs)
    y = jnp.moveaxis(y, 0, 1)
    ym = jnp.mean(y, -1, keepdims=True)
    yv = jnp.mean(jnp.square(y - ym), -1, keepdims=True)
    y = ((y - ym) * lax.rsqrt(yv + WKV_GN_EPS)).reshape(bsz, seq, A_WIDTH)
    y = y * gn_w.astype(f32) + gn_b.astype(f32)
    bonus = jnp.sum(r_h * k_h * r_k.astype(f32), -1, keepdims=True) * v_h
    y = y + bonus.reshape(bsz, seq, A_WIDTH)
    return (y * g.astype(f32)).astype(h.dtype)


def swa_sink_attention(q, k, v, sinks):
    bsz, seq = q.shape[0], q.shape[1]
    nb = seq // BLOCK
    qb = q.reshape(bsz, nb, BLOCK, B_KV_HEADS, B_GROUP, HEAD_DIM)
    kb = k.reshape(bsz, nb, BLOCK, B_KV_HEADS, HEAD_DIM)
    vb = v.reshape(bsz, nb, BLOCK, B_KV_HEADS, HEAD_DIM)
    with_prev = lambda t: jnp.concatenate(
        [jnp.concatenate([jnp.zeros_like(t[:, :1]), t[:, :-1]], axis=1), t], axis=2)
    kw, vw = with_prev(kb), with_prev(vb)
    s = jnp.einsum('bnqhgd,bnkhd->bnhgqk', qb, kw).astype(jnp.float32) * HEAD_DIM ** -0.5
    rel = (jnp.arange(BLOCK)[:, None] + BLOCK) - jnp.arange(2 * BLOCK)[None, :]
    band = (rel >= 0) & (rel < WINDOW)
    key_ok = (jnp.arange(nb)[:, None] * BLOCK - BLOCK + jnp.arange(2 * BLOCK)[None, :]) >= 0
    mask = band[None] & key_ok[:, None, :]
    s = jnp.where(mask[None, :, None, None], s, -jnp.inf)
    sink = sinks.astype(jnp.float32).reshape(1, 1, B_KV_HEADS, B_GROUP, 1, 1)
    m = jnp.maximum(jnp.max(s, -1, keepdims=True), sink)
    e = jnp.exp(s - m)
    prob = e / (jnp.sum(e, -1, keepdims=True) + jnp.exp(sink - m))
    out = jnp.einsum('bnhgqk,bnkhd->bnqhgd', prob.astype(v.dtype), vw)
    return out.reshape(bsz, seq, B_WIDTH)


def diff_attention(q, k, v, lam, lam_init, subln_g):
    bsz, seq = q.shape[0], q.shape[1]
    nb = seq // BLOCK
    scale = DIFF_HEAD_DIM ** -0.5
    q_blocks = jnp.moveaxis(q.reshape(bsz, nb, BLOCK, DIFF_HEADS, 2, DIFF_HEAD_DIM), 1, 0)
    key_pos = jnp.arange(seq)

    def one_block(args):
        q_blk, n = args
        s = jnp.einsum('bqhcd,bkhcd->bchqk', q_blk, k).astype(jnp.float32) * scale
        q_pos = n * BLOCK + jnp.arange(BLOCK)
        s = jnp.where(key_pos[None, :] <= q_pos[:, None], s, -jnp.inf)
        prob = jax.nn.softmax(s, axis=-1)
        attn = prob[:, 0] - lam * prob[:, 1]
        return jnp.einsum('bhqk,bkhe->bqhe', attn.astype(v.dtype), v)

    out = lax.map(one_block, (q_blocks, jnp.arange(nb)))
    out = jnp.moveaxis(out, 0, 1).reshape(bsz, seq, DIFF_HEADS, 2 * DIFF_HEAD_DIM).astype(jnp.float32)
    out = out * lax.rsqrt(jnp.mean(jnp.square(out), -1, keepdims=True) + 1e-5)
    out = out * subln_g.astype(jnp.float32) * (1.0 - lam_init)
    return out.reshape(bsz, seq, DIFF_WIDTH).astype(v.dtype)


def swiglu(x, w1, w3, w2):
    return (jax.nn.silu(x @ w1) * (x @ w3)) @ w2


def setup_inputs(seed: int = 0) -> dict:
    key = jax.random.key(seed)
    ks = iter(jax.random.split(key, 48))
    f32 = jnp.float32
    n_even = (DEPTH + 1) // 2
    n_odd = DEPTH // 2
    beta = (8.0 * DEPTH) ** -0.25
    nrm = lambda shape, scale: jax.random.normal(next(ks), shape, f32) * scale
    unif = lambda shape, lo, hi: jax.random.uniform(next(ks), shape, f32, lo, hi)
    x = nrm((BATCH, SEQ, D_MODEL), 1.0)
    p = nrm((DEPTH, BATCH, SEQ, PLE_DIM), 1.0)
    positions = (jnp.arange(SEQ, dtype=jnp.int32)[None, :]
                 + jax.random.randint(next(ks), (BATCH, 1), 0, 1024, jnp.int32))
    return {
        'x': x,
        'p': p,
        'positions': positions,
        'e_w_in': nrm((n_even, D_MODEL, EVEN_COLS), D_MODEL ** -0.5),
        'e_mu': unif((n_even, A_COLS), 0.0, 1.0),
        'e_w0': unif((n_even, A_WIDTH), -6.0, 1.0),
        'e_w_up': nrm((n_even, DECAY_LORA, A_WIDTH), 0.1),
        'e_a0': nrm((n_even, A_WIDTH), 0.1),
        'e_a_up': nrm((n_even, AAA_LORA, A_WIDTH), AAA_LORA ** -0.5),
        'e_g_up': nrm((n_even, GATE_LORA, A_WIDTH), GATE_LORA ** -0.5),
        'e_k_k': 0.85 + nrm((n_even, A_WIDTH), 0.05),
        'e_k_a': 1.0 + nrm((n_even, A_WIDTH), 0.05),
        'e_r_k': nrm((n_even, A_HEADS, HEAD_DIM), 0.1),
        'e_gn_w': 1.0 + nrm((n_even, A_WIDTH), 0.05),
        'e_gn_b': nrm((n_even, A_WIDTH), 0.02),
        'e_sinks': nrm((n_even, B_Q_HEADS), 1.0),
        'e_w_out': nrm((n_even, EVEN_MIX, D_MODEL), EVEN_MIX ** -0.5 * beta),
        'o_w_in': nrm((n_odd, D_MODEL, ODD_COLS), D_MODEL ** -0.5),
        'o_lambda': nrm((n_odd, 4, DIFF_HEAD_DIM), 0.1),
        'o_subln_g': 1.0 + nrm((n_odd, 2 * DIFF_HEAD_DIM), 0.05),
        'o_w_out': nrm((n_odd, DIFF_WIDTH, D_MODEL), DIFF_WIDTH ** -0.5 * beta),
        'ln1_g': 1.0 + nrm((DEPTH, D_MODEL), 0.05),
        'ln1_b': nrm((DEPTH, D_MODEL), 0.02),
        'ln2_g': 1.0 + nrm((DEPTH, D_MODEL), 0.05),
        'ln2_b': nrm((DEPTH, D_MODEL), 0.02),
        'ffn_w1': nrm((DEPTH, D_MODEL, FFN_DIM), D_MODEL ** -0.5),
        'ffn_w3': nrm((DEPTH, D_MODEL, FFN_DIM), D_MODEL ** -0.5),
        'ffn_w2': nrm((DEPTH, FFN_DIM, D_MODEL), FFN_DIM ** -0.5 * beta),
        'ple_w_proj': nrm((DEPTH, PLE_DIM, D_MODEL), PLE_DIM ** -0.5),
        'ple_w_gate': nrm((DEPTH, D_MODEL, D_MODEL), D_MODEL ** -0.5),
    }


def reference(x, p, positions, e_w_in, e_mu, e_w0, e_w_up, e_a0, e_a_up, e_g_up,
              e_k_k, e_k_a, e_r_k, e_gn_w, e_gn_b, e_sinks, e_w_out,
              o_w_in, o_lambda, o_subln_g, o_w_out,
              ln1_g, ln1_b, ln2_g, ln2_b, ffn_w1, ffn_w3, ffn_w2,
              ple_w_proj, ple_w_gate):
    bsz, seq = x.shape[0], x.shape[1]
    alpha = (2.0 * DEPTH) ** 0.25
    for i in range(DEPTH):
        j = i // 2
        if i % 2 == 0:
            h = x @ e_w_in[j]
            ha, hq, hk, hv = jnp.split(
                h, [A_COLS, A_COLS + B_WIDTH, A_COLS + B_WIDTH + B_KV_WIDTH], axis=-1)
            ya = rwkv7_mix(ha, e_mu[j], e_w0[j], e_w_up[j], e_a0[j], e_a_up[j], e_g_up[j],
                           e_k_k[j], e_k_a[j], e_r_k[j], e_gn_w[j], e_gn_b[j])
            q = partial_rotary(hq.reshape(bsz, seq, B_Q_HEADS, HEAD_DIM), positions)
            k = partial_rotary(hk.reshape(bsz, seq, B_KV_HEADS, HEAD_DIM), positions)
            v = hv.reshape(bsz, seq, B_KV_HEADS, HEAD_DIM)
            yb = swa_sink_attention(q, k, v, e_sinks[j])
            mix = jnp.concatenate([ya, yb], axis=-1) @ e_w_out[j]
        else:
            h = x @ o_w_in[j]
            hq, hk, hv = jnp.split(h, [DIFF_WIDTH, 2 * DIFF_WIDTH], axis=-1)
            q = partial_rotary(hq.reshape(bsz, seq, 2 * DIFF_HEADS, DIFF_HEAD_DIM), positions)
            k = partial_rotary(hk.reshape(bsz, seq, 2 * DIFF_HEADS, DIFF_HEAD_DIM), positions)
            q = q.reshape(bsz, seq, DIFF_HEADS, 2, DIFF_HEAD_DIM)
            k = k.reshape(bsz, seq, DIFF_HEADS, 2, DIFF_HEAD_DIM)
            v = hv.reshape(bsz, seq, DIFF_HEADS, 2 * DIFF_HEAD_DIM)
            lam_init = 0.8 - 0.6 * math.exp(-0.3 * i)
            lv = o_lambda[j].astype(jnp.float32)
            lam = jnp.exp(jnp.sum(lv[0] * lv[1])) - jnp.exp(jnp.sum(lv[2] * lv[3])) + lam_init
            mix = diff_attention(q, k, v, lam, lam_init, o_subln_g[j]) @ o_w_out[j]
        x = layer_norm(alpha * x + mix, ln1_g[i], ln1_b[i])
        x = layer_norm(alpha * x + swiglu(x, ffn_w1[i], ffn_w3[i], ffn_w2[i]), ln2_g[i], ln2_b[i])
        x = x + jax.nn.sigmoid(x @ ple_w_gate[i]) * (p[i] @ ple_w_proj[i])
    return x
```

```python
import functools
import math

import jax
import jax.numpy as jnp
from jax import lax
from jax.experimental import pallas as pl
from jax.experimental.pallas import tpu as pltpu

F32 = jnp.float32
BF16 = jnp.bfloat16
HI = lax.Precision.HIGHEST

HEAD_DIM = 64
DIFF_HEAD_DIM = 128
WINDOW = 128
ROPE_THETA = 500000.0
ROPE_FRACTION = 4
LN_EPS = 1e-5
WKV_GN_EPS = 64e-5
SUBLN_EPS = 1e-5
RWKV_CHUNK = 64
VMEM_LIMIT = 56 * 1024 * 1024


def _dot(a, b, prec=None):
    return jnp.dot(a, b, preferred_element_type=F32, precision=prec)


def _dot_nt(a, b, prec=None):
    return lax.dot_general(a, b, (((1,), (1,)), ((), ())), preferred_element_type=F32, precision=prec)


def _dot_tn(a, b, prec=None):
    return lax.dot_general(a, b, (((0,), (0,)), ((), ())), preferred_element_type=F32, precision=prec)


def _params(*sem):
    return pltpu.CompilerParams(dimension_semantics=sem, vmem_limit_bytes=VMEM_LIMIT)


def _layer_norm(z, g, b):
    mu = jnp.mean(z, -1, keepdims=True)
    d = z - mu
    var = jnp.mean(d * d, -1, keepdims=True)
    return d * lax.rsqrt(var + LN_EPS) * g + b


def _mm_kernel(x_ref, w_ref, o_ref):
    o_ref[...] = _dot(x_ref[...], w_ref[...]).astype(o_ref.dtype)


def _matmul(x, w, out_dtype, tm, tn, name):
    t, k = x.shape
    n = w.shape[1]
    return pl.pallas_call(
        _mm_kernel,
        grid=(t // tm, n // tn),
        in_specs=[pl.BlockSpec((tm, k), lambda i, j: (i, 0)),
                  pl.BlockSpec((k, tn), lambda i, j: (0, j))],
        out_specs=pl.BlockSpec((tm, tn), lambda i, j: (i, j)),
        out_shape=jax.ShapeDtypeStruct((t, n), out_dtype),
        compiler_params=_params("parallel", "arbitrary"),
        name=name,
    )(x, w)


def _rwkv_prep_kernel(h_ref, hp_ref, mu_ref, w0_ref, a0_ref, kk_ref, ka_ref, wwa_ref, gup_ref,
                      r_out, lw_out, k_out, v_out, kk_out, a_out, g_out, *, blocks_per_seq, aw):
    i = pl.program_id(0)
    h = h_ref[...]
    tm = h.shape[0]
    first = (i % blocks_per_seq) == 0
    prev_row = jnp.where(first, 0.0, hp_ref[7:8, :])
    shifted = pltpu.roll(h, 1, axis=0)
    row = lax.broadcasted_iota(jnp.int32, h.shape, 0)
    shifted = jnp.where(row == 0, prev_row, shifted)
    hm = h + (shifted - h) * mu_ref[...]
    r = hm[:, 0:aw]
    k = hm[:, aw:2 * aw]
    v = hm[:, 2 * aw:3 * aw]
    z = hm[:, 3 * aw:3 * aw + 128]
    lane = lax.broadcasted_iota(jnp.int32, z.shape, 1)
    z = jnp.where(lane < 64, jnp.tanh(z), z)
    wa = _dot(z, wwa_ref[...], HI)
    w = -jax.nn.softplus(-(w0_ref[...] + wa[:, :aw])) - 0.5
    a = jax.nn.sigmoid(a0_ref[...] + wa[:, aw:])
    lg = hm[:, 3 * aw + 128:]
    g = _dot(jax.nn.sigmoid(lg), gup_ref[...], HI)
    r_out[...] = r
    lw_out[...] = -jnp.exp(w)
    k_out[...] = k * (1.0 + (a - 1.0) * ka_ref[...])
    v_out[...] = v
    kk_out[...] = k * kk_ref[...]
    a_out[...] = a
    g_out[...] = g


def _rwkv_prep(h, mu, w0, a0, k_k, k_a, wwa, gup, seq, tm):
    t, cols = h.shape
    aw = w0.shape[1]
    row_spec = pl.BlockSpec((tm, aw), lambda i: (i, 0))
    full = lambda arr: pl.BlockSpec(arr.shape, lambda i: (0, 0))
    kern = functools.partial(_rwkv_prep_kernel, blocks_per_seq=seq // tm, aw=aw)
    return pl.pallas_call(
        kern,
        grid=(t // tm,),
        in_specs=[pl.BlockSpec((tm, cols), lambda i: (i, 0)),
                  pl.BlockSpec((8, cols), lambda i: (jnp.maximum(i * (tm // 8) - 1, 0), 0)),
                  full(mu), full(w0), full(a0), full(k_k), full(k_a), full(wwa), full(gup)],
        out_specs=[row_spec] * 7,
        out_shape=[jax.ShapeDtypeStruct((t, aw), F32)] * 7,
        compiler_params=_params("parallel"),
        name="rwkv_prep",
    )(h, h, mu, w0, a0, k_k, k_a, wwa, gup)


def _rwkv_scan_kernel(r_ref, lw_ref, k_ref, v_ref, kk_ref, a_ref, g_ref, rk_ref, gnw_ref, gnb_ref,
                      o_ref, st_ref, *, heads, n):
    c = r_ref.shape[0]

    @pl.when(pl.program_id(1) == 0)
    def _():
        st_ref[...] = jnp.zeros_like(st_ref)

    row = lax.broadcasted_iota(jnp.int32, (c, c), 0)
    col = lax.broadcasted_iota(jnp.int32, (c, c), 1)
    incl = row >= col
    strict = row > col
    eye_c = (row == col).astype(F32)
    rn = lax.broadcasted_iota(jnp.int32, (n, n), 0)
    cn = lax.broadcasted_iota(jnp.int32, (n, n), 1)
    eye_n = (rn == cn).astype(F32)

    lw = lw_ref[...]
    cs = _dot(incl.astype(F32), lw, HI)
    cs_last = cs[c - 1:c, :]
    w_inc = jnp.exp(cs)
    w_exc = jnp.exp(cs - lw)
    w_inv = jnp.exp(-cs)
    w_rem = jnp.exp(cs_last - cs)
    w_all = jnp.exp(cs_last)

    for h in range(heads):
        sl = slice(h * n, (h + 1) * n)
        kkr = kk_ref[:, sl]
        nrm = jnp.sqrt(jnp.sum(kkr * kkr, -1, keepdims=True))
        kk = kkr / jnp.maximum(nrm, 1e-12)
        rh = r_ref[:, sl]
        kh = k_ref[:, sl]
        vh = v_ref[:, sl]
        bb = kk * a_ref[:, sl]
        a_t = -kk * w_exc[:, sl]
        b_t = bb * w_inv[:, sl]
        k_t = kh * w_inv[:, sl]
        r_t = rh * w_inc[:, sl]
        b_h = bb * w_rem[:, sl]
        k_h = kh * w_rem[:, sl]

        l_ab = jnp.where(strict, _dot_nt(a_t, b_t, HI), 0.0)
        l_ak = jnp.where(strict, _dot_nt(a_t, k_t, HI), 0.0)
        m_rb = jnp.where(incl, _dot_nt(r_t, b_t, HI), 0.0)
        m_rk = jnp.where(incl, _dot_nt(r_t, k_t, HI), 0.0)

        tinv = eye_c + l_ab
        pw = l_ab
        for _ in range(int(math.log2(c)) - 1):
            pw = _dot(pw, pw, HI)
            tinv = tinv + _dot(tinv, pw, HI)

        st = st_ref[h]
        x1 = _dot(a_t, st, HI) + _dot(l_ak, vh, HI)
        u = _dot(tinv, x1, HI)
        y = _dot(r_t, st, HI) + _dot(m_rb, u, HI) + _dot(m_rk, vh, HI)
        st_ref[h] = (_dot_tn(eye_n * w_all[:, sl], st, HI) + _dot_tn(b_h, u, HI)
                     + _dot_tn(k_h, vh, HI))

        ym = jnp.mean(y, -1, keepdims=True)
        d = y - ym
        yv = jnp.mean(d * d, -1, keepdims=True)
        yn = d * lax.rsqrt(yv + WKV_GN_EPS) * gnw_ref[:, sl] + gnb_ref[:, sl]
        bonus = jnp.sum(rh * kh * rk_ref[:, sl], -1, keepdims=True) * vh
        o_ref[:, sl] = ((yn + bonus) * g_ref[:, sl]).astype(o_ref.dtype)


def _rwkv_scan(r, lw, k, v, kk, a, g, r_k, gn_w, gn_b, batch, seq):
    t, aw = r.shape
    c = RWKV_CHUNK
    n = HEAD_DIM
    heads = aw // n
    nchunk = seq // c
    tok = pl.BlockSpec((c, aw), lambda b, j: (b * nchunk + j, 0))
    par = pl.BlockSpec((1, aw), lambda b, j: (0, 0))
    kern = functools.partial(_rwkv_scan_kernel, heads=heads, n=n)
    return pl.pallas_call(
        kern,
        grid=(batch, nchunk),
        in_specs=[tok] * 7 + [par] * 3,
        out_specs=tok,
        out_shape=jax.ShapeDtypeStruct((t, aw), F32),
        scratch_shapes=[pltpu.VMEM((heads, n, n), F32)],
        compiler_params=_params("parallel", "arbitrary"),
        name="rwkv_scan",
    )(r, lw, k, v, kk, a, g, r_k, gn_w, gn_b)


def _rope_tables(positions, head_dim, lanes):
    rd = head_dim // ROPE_FRACTION
    half = rd // 2
    inv_freq = ROPE_THETA ** (-jnp.arange(half, dtype=F32) / half)
    ang = positions.astype(F32).reshape(-1, 1) * inv_freq
    cos, sin = jnp.cos(ang), jnp.sin(ang)
    t = ang.shape[0]
    pad = jnp.zeros((t, head_dim - rd), F32)
    zero = jnp.zeros((t, half), F32)
    c = jnp.concatenate([cos, cos, pad + 1.0], -1)
    sn = jnp.concatenate([-sin, zero, pad], -1)
    sp = jnp.concatenate([zero, sin, pad], -1)
    rep = lanes // head_dim
    return tuple(jnp.tile(x, (1, rep)) for x in (c, sn, sp)), half


def _rope(x, c, sn, sp, half):
    width = x.shape[1]
    rep = width // c.shape[1]
    if rep > 1:
        c, sn, sp = (jnp.concatenate([y] * rep, axis=1) for y in (c, sn, sp))
    nxt = pltpu.roll(x, width - half, axis=1)
    prv = pltpu.roll(x, half, axis=1)
    return x * c + nxt * sn + prv * sp


def _swa_prep_kernel(q_ref, kv_ref, c_ref, sn_ref, sp_ref, q_out, k_out, v_out, *, half, scale):
    c, sn, sp = c_ref[...], sn_ref[...], sp_ref[...]
    q_out[...] = (_rope(q_ref[...], c, sn, sp, half) * scale).astype(q_out.dtype)
    kv = kv_ref[...]
    kw = kv.shape[1] // 2
    k_out[...] = _rope(kv[:, :kw], c, sn, sp, half).astype(k_out.dtype)
    v_out[...] = kv[:, kw:].astype(v_out.dtype)


def _swa_prep(hb, tabs, half, qw, kvw, tm):
    t = hb.shape[0]
    kern = functools.partial(_swa_prep_kernel, half=half, scale=HEAD_DIM ** -0.5)
    tab_spec = pl.BlockSpec((tm, 128), lambda i: (i, 0))
    return pl.pallas_call(
        kern,
        grid=(t // tm,),
        in_specs=[pl.BlockSpec((tm, qw), lambda i: (i, 0)),
                  pl.BlockSpec((tm, 2 * kvw), lambda i: (i, qw // (2 * kvw))),
                  tab_spec, tab_spec, tab_spec],
        out_specs=[pl.BlockSpec((tm, qw), lambda i: (i, 0)),
                   pl.BlockSpec((tm, kvw), lambda i: (i, 0)),
                   pl.BlockSpec((tm, kvw), lambda i: (i, 0))],
        out_shape=[jax.ShapeDtypeStruct((t, qw), BF16),
                   jax.ShapeDtypeStruct((t, kvw), BF16),
                   jax.ShapeDtypeStruct((t, kvw), BF16)],
        compiler_params=_params("parallel"),
        name="swa_prep",
    )(hb, hb, *tabs)


def _swa_kernel(sink_ref, q_ref, kp_ref, kc_ref, vp_ref, vc_ref, o_ref, *, q_heads, kv_heads, n):
    blk = q_ref.shape[0]
    j = pl.program_id(1)
    kw = jnp.concatenate([kp_ref[...], kc_ref[...]], axis=0)
    vw = jnp.concatenate([vp_ref[...], vc_ref[...]], axis=0)
    qi = lax.broadcasted_iota(jnp.int32, (blk, 2 * blk), 0)
    kj = lax.broadcasted_iota(jnp.int32, (blk, 2 * blk), 1)
    rel = qi + blk - kj
    mask = (rel >= 0) & (rel < WINDOW) & ((kj >= blk) | (j > 0))
    group = q_heads // kv_heads
    for h in range(q_heads):
        g = h // group
        qh = q_ref[:, h * n:(h + 1) * n]
        s = _dot_nt(qh, kw[:, g * n:(g + 1) * n])
        s = jnp.where(mask, s, -jnp.inf)
        sink = sink_ref[h]
        m = jnp.maximum(jnp.max(s, -1, keepdims=True), sink)
        e = jnp.exp(s - m)
        den = jnp.sum(e, -1, keepdims=True) + jnp.exp(sink - m)
        p = (e / den).astype(vw.dtype)
        o_ref[:, h * n:(h + 1) * n] = _dot(p, vw[:, g * n:(g + 1) * n]).astype(o_ref.dtype)


def _swa(q, k, v, sinks, batch, seq):
    t, qw = q.shape
    kvw = k.shape[1]
    n = HEAD_DIM
    blk = WINDOW
    nb = seq // blk
    cur = lambda b, j: (b * nb + j, 0)
    prev = lambda b, j: (b * nb + jnp.maximum(j - 1, 0), 0)
    kern = functools.partial(_swa_kernel, q_heads=qw // n, kv_heads=kvw // n, n=n)
    return pl.pallas_call(
        kern,
        grid=(batch, nb),
        in_specs=[pl.BlockSpec(memory_space=pltpu.SMEM),
                  pl.BlockSpec((blk, qw), cur),
                  pl.BlockSpec((blk, kvw), prev), pl.BlockSpec((blk, kvw), cur),
                  pl.BlockSpec((blk, kvw), prev), pl.BlockSpec((blk, kvw), cur)],
        out_specs=pl.BlockSpec((blk, qw), cur),
        out_shape=jax.ShapeDtypeStruct((t, qw), F32),
        compiler_params=_params("parallel", "arbitrary"),
        name="swa",
    )(sinks, q, k, k, v, v)


def _diff_prep_kernel(q_ref, k_ref, v_ref, c_ref, sn_ref, sp_ref, q_out, k_out, v_out, *, half, scale):
    c, sn, sp = c_ref[...], sn_ref[...], sp_ref[...]
    q_out[...] = (_rope(q_ref[...], c, sn, sp, half) * scale).astype(q_out.dtype)
    k_out[...] = _rope(k_ref[...], c, sn, sp, half).astype(k_out.dtype)
    v_out[...] = v_ref[...].astype(v_out.dtype)


def _diff_prep(h, tabs, half, width, tm):
    t = h.shape[0]
    kern = functools.partial(_diff_prep_kernel, half=half, scale=DIFF_HEAD_DIM ** -0.5)
    tab_spec = pl.BlockSpec((tm, 128), lambda i: (i, 0))
    col = lambda c: pl.BlockSpec((tm, width), lambda i, c=c: (i, c))
    return pl.pallas_call(
        kern,
        grid=(t // tm,),
        in_specs=[col(0), col(1), col(2), tab_spec, tab_spec, tab_spec],
        out_specs=[col(0)] * 3,
        out_shape=[jax.ShapeDtypeStruct((t, width), BF16)] * 3,
        compiler_params=_params("parallel"),
        name="diff_prep",
    )(h, h, h, *tabs)


def _diff_attn_kernel(sc_ref, q_ref, k_ref, v_ref, g_ref, o_ref, m_ref, l_ref, acc_ref, *, bq, bk, dh):
    qi = pl.program_id(2)
    ki = pl.program_id(3)
    nk = pl.num_programs(3)

    @pl.when(ki == 0)
    def _():
        m_ref[...] = jnp.full_like(m_ref, -jnp.inf)
        l_ref[...] = jnp.zeros_like(l_ref)
        acc_ref[...] = jnp.zeros_like(acc_ref)

    @pl.when(ki * bk <= qi * bq + bq - 1)
    def _():
        q_pos = qi * bq + lax.broadcasted_iota(jnp.int32, (bq, bk), 0)
        k_pos = ki * bk + lax.broadcasted_iota(jnp.int32, (bq, bk), 1)
        causal = k_pos <= q_pos
        v = v_ref[...]
        for c in range(2):
            s = _dot_nt(q_ref[:, c * dh:(c + 1) * dh], k_ref[:, c * dh:(c + 1) * dh])
            s = jnp.where(causal, s, -jnp.inf)
            m_old = m_ref[c]
            m_new = jnp.maximum(m_old, jnp.max(s, -1, keepdims=True))
            alpha = jnp.exp(m_old - m_new)
            p = jnp.exp(s - m_new[:, 0:1])
            l_ref[c] = alpha * l_ref[c] + jnp.sum(p, -1, keepdims=True)
            acc_ref[c] = alpha[:, 0:1] * acc_ref[c] + _dot(p.astype(v.dtype), v)
            m_ref[c] = m_new

    @pl.when(ki == nk - 1)
    def _():
        lam = sc_ref[0]
        post = sc_ref[1]
        o1 = acc_ref[0] / l_ref[0][:, 0:1]
        o2 = acc_ref[1] / l_ref[1][:, 0:1]
        out = o1 - lam * o2
        out = out * lax.rsqrt(jnp.mean(out * out, -1, keepdims=True) + SUBLN_EPS)
        o_ref[...] = (out * g_ref[...] * post).astype(o_ref.dtype)


def _diff_attn(q, k, v, scalars, subln_g, batch, seq, bq, bk):
    t, width = q.shape
    dh = DIFF_HEAD_DIM
    hw = 2 * dh
    heads = width // hw
    nq, nk = seq // bq, seq // bk

    def kv_map(b, h, qi, ki):
        last = (qi * bq + bq - 1) // bk
        return (b * nk + jnp.minimum(ki, last), h)

    kern = functools.partial(_diff_attn_kernel, bq=bq, bk=bk, dh=dh)
    return pl.pallas_call(
        kern,
        grid=(batch, heads, nq, nk),
        in_specs=[pl.BlockSpec(memory_space=pltpu.SMEM),
                  pl.BlockSpec((bq, hw), lambda b, h, qi, ki: (b * nq + qi, h)),
                  pl.BlockSpec((bk, hw), kv_map),
                  pl.BlockSpec((bk, hw), kv_map),
                  pl.BlockSpec((1, hw), lambda b, h, qi, ki: (0, 0))],
        out_specs=pl.BlockSpec((bq, hw), lambda b, h, qi, ki: (b * nq + qi, h)),
        out_shape=jax.ShapeDtypeStruct((t, width), BF16),
        scratch_shapes=[pltpu.VMEM((2, bq, 128), F32), pltpu.VMEM((2, bq, 128), F32),
                        pltpu.VMEM((2, bq, hw), F32)],
        compiler_params=_params("parallel", "parallel", "parallel", "arbitrary"),
        name="diff_attn",
    )(scalars, q, k, v, subln_g)


def _outproj_ln_kernel(*refs, n_in, alpha):
    ys = refs[:n_in]
    ws = refs[n_in:2 * n_in]
    x_ref, g_ref, b_ref, o_ref, obf_ref = refs[2 * n_in:]
    mix = _dot(ys[0][...].astype(BF16), ws[0][...])
    for y, w in zip(ys[1:], ws[1:]):
        mix = mix + _dot(y[...].astype(BF16), w[...])
    out = _layer_norm(alpha * x_ref[...] + mix, g_ref[...], b_ref[...])
    o_ref[...] = out
    obf_ref[...] = out.astype(BF16)


def _outproj_ln(ys, ws, x, g, b, alpha, tm):
    t, d = x.shape
    n_in = len(ys)
    kern = functools.partial(_outproj_ln_kernel, n_in=n_in, alpha=alpha)
    row = lambda width: pl.BlockSpec((tm, width), lambda i: (i, 0))
    full = lambda arr: pl.BlockSpec(arr.shape, lambda i: (0, 0))
    return pl.pallas_call(
        kern,
        grid=(t // tm,),
        in_specs=[row(y.shape[1]) for y in ys] + [full(w) for w in ws] + [row(d), full(g), full(b)],
        out_specs=[row(d), row(d)],
        out_shape=[jax.ShapeDtypeStruct((t, d), F32), jax.ShapeDtypeStruct((t, d), BF16)],
        compiler_params=_params("parallel"),
        name="outproj_ln",
    )(*ys, *ws, x, g, b)


def _ffn_up_kernel(x_ref, w1_ref, w3_ref, o_ref):
    x = x_ref[...]
    h1 = _dot(x, w1_ref[...])
    h3 = _dot(x, w3_ref[...])
    o_ref[...] = (h1 * jax.nn.sigmoid(h1) * h3).astype(o_ref.dtype)


def _ffn_up(x, w1, w3, tm, tf):
    t, d = x.shape
    f = w1.shape[1]
    return pl.pallas_call(
        _ffn_up_kernel,
        grid=(t // tm, f // tf),
        in_specs=[pl.BlockSpec((tm, d), lambda i, j: (i, 0)),
                  pl.BlockSpec((d, tf), lambda i, j: (0, j)),
                  pl.BlockSpec((d, tf), lambda i, j: (0, j))],
        out_specs=pl.BlockSpec((tm, tf), lambda i, j: (i, j)),
        out_shape=jax.ShapeDtypeStruct((t, f), BF16),
        compiler_params=_params("parallel", "arbitrary"),
        name="ffn_up",
    )(x, w1, w3)


def _ffn_down_ln_kernel(a_ref, w_ref, x_ref, g_ref, b_ref, o_ref, obf_ref, acc_ref, *, alpha):
    kstep = pl.program_id(1)

    @pl.when(kstep == 0)
    def _():
        acc_ref[...] = jnp.zeros_like(acc_ref)

    acc_ref[...] += _dot(a_ref[...], w_ref[...])

    @pl.when(kstep == pl.num_programs(1) - 1)
    def _():
        out = _layer_norm(alpha * x_ref[...] + acc_ref[...], g_ref[...], b_ref[...])
        o_ref[...] = out
        obf_ref[...] = out.astype(BF16)


def _ffn_down_ln(act, w2, x, g, b, alpha, tm, tk):
    t, d = x.shape
    f = act.shape[1]
    kern = functools.partial(_ffn_down_ln_kernel, alpha=alpha)
    row = pl.BlockSpec((tm, d), lambda i, j: (i, 0))
    par = pl.BlockSpec((1, d), lambda i, j: (0, 0))
    return pl.pallas_call(
        kern,
        grid=(t // tm, f // tk),
        in_specs=[pl.BlockSpec((tm, tk), lambda i, j: (i, j)),
                  pl.BlockSpec((tk, d), lambda i, j: (j, 0)),
                  row, par, par],
        out_specs=[row, row],
        out_shape=[jax.ShapeDtypeStruct((t, d), F32), jax.ShapeDtypeStruct((t, d), BF16)],
        scratch_shapes=[pltpu.VMEM((tm, d), F32)],
        compiler_params=_params("parallel", "arbitrary"),
        name="ffn_down_ln",
    )(act, w2, x, g, b)


def _ple_kernel(xbf_ref, wg_ref, p_ref, wp_ref, x_ref, o_ref, obf_ref):
    gate = jax.nn.sigmoid(_dot(xbf_ref[...], wg_ref[...]))
    proj = _dot(p_ref[...].astype(BF16), wp_ref[...])
    out = x_ref[...] + gate * proj
    o_ref[...] = out
    obf_ref[...] = out.astype(BF16)


def _ple(xbf, wg, p, wp, x, tm, tn):
    t, d = x.shape
    pd = p.shape[1]
    out = pl.BlockSpec((tm, tn), lambda i, j: (i, j))
    return pl.pallas_call(
        _ple_kernel,
        grid=(t // tm, d // tn),
        in_specs=[pl.BlockSpec((tm, d), lambda i, j: (i, 0)),
                  pl.BlockSpec((d, tn), lambda i, j: (0, j)),
                  pl.BlockSpec((tm, pd), lambda i, j: (i, 0)),
                  pl.BlockSpec((pd, tn), lambda i, j: (0, j)),
                  out],
        out_specs=[out, out],
        out_shape=[jax.ShapeDtypeStruct((t, d), F32), jax.ShapeDtypeStruct((t, d), BF16)],
        compiler_params=_params("parallel", "arbitrary"),
        name="ple",
    )(xbf, wg, p, wp, x)


def _even_mixer(xbf, positions, w_in, mu, w0, w_up, a0, a_up, g_up, k_k, k_a, r_k, gn_w, gn_b,
                sinks, batch, seq):
    aw = w0.shape[0]
    dl = w_up.shape[0]
    al = a_up.shape[0]
    gl = g_up.shape[0]
    a_cols = 3 * aw + dl + al + gl
    gl_pad = 384
    assert dl == 64 and al == 64 and gl <= gl_pad
    d = w_in.shape[0]
    pad_cols = jnp.zeros((d, gl_pad - gl), F32)
    w_a = jnp.concatenate([w_in[:, :a_cols], pad_cols], axis=1).astype(BF16)
    w_b = w_in[:, a_cols:].astype(BF16)
    mu_p = jnp.concatenate([mu, jnp.zeros((gl_pad - gl,), F32)])[None, :]
    zeros = jnp.zeros((dl, aw), F32)
    wwa = jnp.concatenate([jnp.concatenate([w_up, zeros], 1), jnp.concatenate([zeros, a_up], 1)], 0)
    gup = jnp.concatenate([g_up, jnp.zeros((gl_pad - gl, aw), F32)], 0)

    h_a = _matmul(xbf, w_a, F32, 1024, 512, "proj_rwkv")
    h_b = _matmul(xbf, w_b, F32, 1024, w_b.shape[1], "proj_swa")

    r, lw, k, v, kk, a, g = _rwkv_prep(h_a, mu_p, w0[None], a0[None], k_k[None], k_a[None], wwa, gup,
                                       seq, 256)
    ya = _rwkv_scan(r, lw, k, v, kk, a, g, r_k.reshape(1, aw), gn_w[None], gn_b[None], batch, seq)

    qw = sinks.shape[0] * HEAD_DIM
    kvw = (w_b.shape[1] - qw) // 2
    tabs, half = _rope_tables(positions, HEAD_DIM, 128)
    q, kb, vb = _swa_prep(h_b, tabs, half, qw, kvw, 512)
    yb = _swa(q, kb, vb, sinks, batch, seq)
    return ya, yb


def _odd_mixer(xbf, positions, w_in, lam_p, subln_g, layer_idx, batch, seq):
    width = w_in.shape[1] // 3
    h = _matmul(xbf, w_in.astype(BF16), F32, 1024, 1024, "proj_diff")
    tabs, half = _rope_tables(positions, DIFF_HEAD_DIM, 128)
    q, k, v = _diff_prep(h, tabs, half, width, 256)
    lam_init = 0.8 - 0.6 * math.exp(-0.3 * layer_idx)
    lv = lam_p.astype(F32)
    lam = jnp.exp(jnp.sum(lv[0] * lv[1])) - jnp.exp(jnp.sum(lv[2] * lv[3])) + lam_init
    scalars = jnp.stack([lam, jnp.asarray(1.0 - lam_init, F32)]).astype(F32)
    return _diff_attn(q, k, v, scalars, subln_g[None], batch, seq, 512, 512)


def kernel(x, p, positions, e_w_in, e_mu, e_w0, e_w_up, e_a0, e_a_up, e_g_up, e_k_k, e_k_a, e_r_k,
           e_gn_w, e_gn_b, e_sinks, e_w_out, o_w_in, o_lambda, o_subln_g, o_w_out,
           ln1_g, ln1_b, ln2_g, ln2_b, ffn_w1, ffn_w3, ffn_w2, ple_w_proj, ple_w_gate):
    batch, seq, d = x.shape
    depth = p.shape[0]
    t = batch * seq
    alpha = (2.0 * depth) ** 0.25
    xf = x.reshape(t, d)
    xbf = xf.astype(BF16)
    for i in range(depth):
        j = i // 2
        if i % 2 == 0:
            ya, yb = _even_mixer(xbf, positions, e_w_in[j], e_mu[j], e_w0[j], e_w_up[j], e_a0[j],
                                 e_a_up[j], e_g_up[j], e_k_k[j], e_k_a[j], e_r_k[j], e_gn_w[j],
                                 e_gn_b[j], e_sinks[j], batch, seq)
            aw = ya.shape[1]
            w_out = e_w_out[j].astype(BF16)
            ys, ws = [ya, yb], [w_out[:aw], w_out[aw:]]
        else:
            y = _odd_mixer(xbf, positions, o_w_in[j], o_lambda[j], o_subln_g[j], i, batch, seq)
            ys, ws = [y], [o_w_out[j].astype(BF16)]
        xf, xbf = _outproj_ln(ys, ws, xf, ln1_g[i][None], ln1_b[i][None], alpha, 256)
        act = _ffn_up(xbf, ffn_w1[i].astype(BF16), ffn_w3[i].astype(BF16), 1024, 512)
        xf, xbf = _ffn_down_ln(act, ffn_w2[i].astype(BF16), xf, ln2_g[i][None], ln2_b[i][None],
                               alpha, 512, 1408)
        xf, xbf = _ple(xbf, ple_w_gate[i].astype(BF16), p[i].reshape(t, -1),
                       ple_w_proj[i].astype(BF16), xf, 512, 1024)
    return xf.reshape(batch, seq, d)
```

```python
import functools
import math

import jax
import jax.numpy as jnp
from jax import lax
from jax.experimental import pallas as pl
from jax.experimental.pallas import tpu as pltpu

F32 = jnp.float32
BF16 = jnp.bfloat16
HI = lax.Precision.HIGHEST

HEAD_DIM = 64
DIFF_HEAD_DIM = 128
WINDOW = 128
ROPE_THETA = 500000.0
ROPE_FRACTION = 4
LN_EPS = 1e-5
WKV_GN_EPS = 64e-5
SUBLN_EPS = 1e-5
RWKV_CHUNK = 64
VMEM_LIMIT = 56 * 1024 * 1024


def _dot(a, b, prec=None):
    return jnp.dot(a, b, preferred_element_type=F32, precision=prec)


def _dot_nt(a, b, prec=None):
    return lax.dot_general(a, b, (((1,), (1,)), ((), ())), preferred_element_type=F32, precision=prec)


def _dot_tn(a, b, prec=None):
    return lax.dot_general(a, b, (((0,), (0,)), ((), ())), preferred_element_type=F32, precision=prec)


def _params(*sem):
    return pltpu.CompilerParams(dimension_semantics=sem, vmem_limit_bytes=VMEM_LIMIT)


def _layer_norm(z, g, b):
    mu = jnp.mean(z, -1, keepdims=True)
    d = z - mu
    var = jnp.mean(d * d, -1, keepdims=True)
    return d * lax.rsqrt(var + LN_EPS) * g + b


def _mm_kernel(x_ref, w_ref, o_ref):
    o_ref[...] = _dot(x_ref[...], w_ref[...]).astype(o_ref.dtype)


def _matmul(x, w, out_dtype, tm, tn, name):
    t, k = x.shape
    n = w.shape[1]
    return pl.pallas_call(
        _mm_kernel,
        grid=(t // tm, n // tn),
        in_specs=[pl.BlockSpec((tm, k), lambda i, j: (i, 0)),
                  pl.BlockSpec((k, tn), lambda i, j: (0, j))],
        out_specs=pl.BlockSpec((tm, tn), lambda i, j: (i, j)),
        out_shape=jax.ShapeDtypeStruct((t, n), out_dtype),
        compiler_params=_params("parallel", "arbitrary"),
        name=name,
    )(x, w)


def _split_dot(x, w):
    hi = x.astype(BF16)
    lo = (x - hi.astype(F32)).astype(BF16)
    return _dot(hi, w) + _dot(lo, w)


def _head_sums(x, ones):
    gw = ones.shape[0]
    parts = [_split_dot(x[:, i:i + gw], ones) for i in range(0, x.shape[1], gw)]
    return jnp.concatenate(parts, axis=1)


def _rwkv_prep_kernel(h_ref, hp_ref, mu_ref, w0_ref, a0_ref, kk_ref, ka_ref, rk_ref, wwa_ref,
                      gup_ref, ones_ref, at_out, bt_out, kt_out, rt_out, bh_out, kh_out, v_out,
                      g_out, bvg_out, wc_out, *, blocks_per_seq, aw, chunk):
    i = pl.program_id(0)
    h = h_ref[...]
    tm = h.shape[0]
    first = (i % blocks_per_seq) == 0
    prev_row = jnp.where(first, 0.0, hp_ref[7:8, :])
    shifted = pltpu.roll(h, 1, axis=0)
    row = lax.broadcasted_iota(jnp.int32, h.shape, 0)
    shifted = jnp.where(row == 0, prev_row, shifted)
    hm = h + (shifted - h) * mu_ref[...]
    r = hm[:, 0:aw]
    k = hm[:, aw:2 * aw]
    v = hm[:, 2 * aw:3 * aw]
    z = hm[:, 3 * aw:3 * aw + 128]
    lane = lax.broadcasted_iota(jnp.int32, z.shape, 1)
    z = jnp.where(lane < 64, jnp.tanh(z), z)
    wa = _dot(z, wwa_ref[...], HI)
    w = -jax.nn.softplus(-(w0_ref[...] + wa[:, :aw])) - 0.5
    a = jax.nn.sigmoid(a0_ref[...] + wa[:, aw:])
    lg = hm[:, 3 * aw + 128:]
    g = _dot(jax.nn.sigmoid(lg), gup_ref[...], HI)
    lw = -jnp.exp(w)

    ri = lax.broadcasted_iota(jnp.int32, (tm, tm), 0)
    ci = lax.broadcasted_iota(jnp.int32, (tm, tm), 1)
    shift = int(math.log2(chunk))
    same = (ri >> shift) == (ci >> shift)
    cs = _dot((same & (ri >= ci)).astype(F32), lw, HI)
    tot = _dot(same.astype(F32), lw, HI)

    ones = ones_ref[...]
    kkr = k * kk_ref[...]
    kk = kkr / jnp.maximum(jnp.sqrt(_head_sums(kkr * kkr, ones)), 1e-12)
    k2 = k * (1.0 + (a - 1.0) * ka_ref[...])
    bb = kk * a
    w_inv = jnp.exp(-cs)
    w_rem = jnp.exp(tot - cs)
    at_out[...] = (-kk * jnp.exp(cs - lw)).astype(BF16)
    bt_out[...] = (bb * w_inv).astype(BF16)
    kt_out[...] = (k2 * w_inv).astype(BF16)
    rt_out[...] = (r * jnp.exp(cs)).astype(BF16)
    bh_out[...] = (bb * w_rem).astype(BF16)
    kh_out[...] = (k2 * w_rem).astype(BF16)
    v_out[...] = v.astype(BF16)
    g_out[...] = g
    bvg_out[...] = _head_sums(r * k2 * rk_ref[...], ones) * v * g
    for c in range(tm // chunk):
        wc_out[c] = jnp.exp(tot[c * chunk:c * chunk + 1, :])


def _rwkv_prep(h, mu, w0, a0, k_k, k_a, r_k, wwa, gup, ones, seq, tm):
    t, cols = h.shape
    aw = w0.shape[1]
    chunk = RWKV_CHUNK
    row_spec = pl.BlockSpec((tm, aw), lambda i: (i, 0))
    full = lambda arr: pl.BlockSpec(arr.shape, lambda i: (0, 0))
    kern = functools.partial(_rwkv_prep_kernel, blocks_per_seq=seq // tm, aw=aw, chunk=chunk)
    return pl.pallas_call(
        kern,
        grid=(t // tm,),
        in_specs=[pl.BlockSpec((tm, cols), lambda i: (i, 0)),
                  pl.BlockSpec((8, cols), lambda i: (jnp.maximum(i * (tm // 8) - 1, 0), 0)),
                  full(mu), full(w0), full(a0), full(k_k), full(k_a), full(r_k), full(wwa),
                  full(gup), full(ones)],
        out_specs=[row_spec] * 9 + [pl.BlockSpec((tm // chunk, 1, aw), lambda i: (i, 0, 0))],
        out_shape=([jax.ShapeDtypeStruct((t, aw), BF16)] * 7 + [jax.ShapeDtypeStruct((t, aw), F32)] * 2
                   + [jax.ShapeDtypeStruct((t // chunk, 1, aw), F32)]),
        compiler_params=_params("parallel"),
        name="rwkv_prep",
    )(h, h, mu, w0, a0, k_k, k_a, r_k, wwa, gup, ones)


def _rwkv_scan_kernel(at_ref, bt_ref, kt_ref, rt_ref, bh_ref, kh_ref, v_ref, wc_ref, y_ref, st_ref,
                      *, pairs, n):
    c = at_ref.shape[0]
    n2 = 2 * n

    @pl.when(pl.program_id(1) == 0)
    def _():
        st_ref[...] = jnp.zeros_like(st_ref)

    row = lax.broadcasted_iota(jnp.int32, (n2, n2), 0)
    col = lax.broadcasted_iota(jnp.int32, (n2, n2), 1)
    same = (row >= c) == (col >= c)
    strict = same & (row > col)
    incl = same & (row >= col)
    incl2 = jnp.concatenate([incl, incl], axis=1)
    eye = (row == col).astype(F32)
    lo = lax.broadcasted_iota(jnp.int32, (c, n2), 1) < n

    def stack(ref, sl):
        x = ref[:, sl]
        zero = jnp.zeros_like(x)
        return jnp.concatenate([jnp.where(lo, x, zero), jnp.where(lo, zero, x)], axis=0)

    sls = [slice(p * n2, (p + 1) * n2) for p in range(pairs)]
    stacks = [[stack(ref, sl) for ref in (at_ref, bt_ref, kt_ref, rt_ref, bh_ref, kh_ref, v_ref)]
              for sl in sls]
    ar = [jnp.concatenate([st[0], st[3]], axis=0) for st in stacks]
    big = [_dot_nt(ar[p], jnp.concatenate([stacks[p][1], stacks[p][2]], axis=0)) for p in range(pairs)]
    l_ab = [jnp.where(strict, b[:n2, :n2], 0.0) for b in big]
    l_ak = [jnp.where(strict, b[:n2, n2:], 0.0).astype(BF16) for b in big]
    m_all = [jnp.where(incl2, b[n2:, :], 0.0).astype(BF16) for b in big]

    tinv = [eye + l for l in l_ab]
    pw = [l.astype(BF16) for l in l_ab]
    for _ in range(int(math.log2(c)) - 1):
        pw = [_dot(x, x).astype(BF16) for x in pw]
        tinv = [t + _dot(t.astype(BF16), x) for t, x in zip(tinv, pw)]
    tinv = [t.astype(BF16) for t in tinv]

    s0 = [st_ref[p] for p in range(pairs)]
    from_state = [_dot_nt(ar[p], s0[p].astype(BF16)) for p in range(pairs)]
    x1 = [(from_state[p][:n2] + _dot(l_ak[p], stacks[p][6])).astype(BF16) for p in range(pairs)]
    u = [_dot(tinv[p], x1[p]).astype(BF16) for p in range(pairs)]
    uv = [jnp.concatenate([u[p], stacks[p][6]], axis=0) for p in range(pairs)]
    for p in range(pairs):
        yst = from_state[p][n2:] + _dot(m_all[p], uv[p])
        st_ref[p] = s0[p] * wc_ref[0, :, sls[p]] + _dot_tn(
            uv[p], jnp.concatenate([stacks[p][4], stacks[p][5]], axis=0))
        y_ref[:, sls[p]] = yst[:c] + yst[c:]


def _rwkv_scan(at, bt, kt, rt, bh, kh, v, wc, batch, seq):
    t, aw = at.shape
    c = RWKV_CHUNK
    n = HEAD_DIM
    assert c == n
    pairs = aw // (2 * n)
    nchunk = seq // c
    tok = pl.BlockSpec((c, aw), lambda b, j: (b * nchunk + j, 0))
    kern = functools.partial(_rwkv_scan_kernel, pairs=pairs, n=n)
    return pl.pallas_call(
        kern,
        grid=(batch, nchunk),
        in_specs=[tok] * 7 + [pl.BlockSpec((1, 1, aw), lambda b, j: (b * nchunk + j, 0, 0))],
        out_specs=tok,
        out_shape=jax.ShapeDtypeStruct((t, aw), F32),
        scratch_shapes=[pltpu.VMEM((pairs, 2 * n, 2 * n), F32)],
        compiler_params=_params("parallel", "arbitrary"),
        name="rwkv_scan",
    )(at, bt, kt, rt, bh, kh, v, wc)


def _rwkv_post_kernel(y_ref, g_ref, bvg_ref, gnw_ref, gnb_ref, ones_ref, o_ref, *, n):
    y = y_ref[...]
    ones = ones_ref[...]
    ym = _head_sums(y, ones) * (1.0 / n)
    d = y - ym
    yv = _head_sums(d * d, ones) * (1.0 / n)
    yn = d * lax.rsqrt(yv + WKV_GN_EPS) * gnw_ref[...] + gnb_ref[...]
    o_ref[...] = (yn * g_ref[...] + bvg_ref[...]).astype(o_ref.dtype)


def _rwkv_post(y, g, bvg, gn_w, gn_b, ones, tm):
    t, aw = y.shape
    row = pl.BlockSpec((tm, aw), lambda i: (i, 0))
    full = lambda arr: pl.BlockSpec(arr.shape, lambda i: (0, 0))
    return pl.pallas_call(
        functools.partial(_rwkv_post_kernel, n=HEAD_DIM),
        grid=(t // tm,),
        in_specs=[row, row, row, full(gn_w), full(gn_b), full(ones)],
        out_specs=row,
        out_shape=jax.ShapeDtypeStruct((t, aw), BF16),
        compiler_params=_params("parallel"),
        name="rwkv_post",
    )(y, g, bvg, gn_w, gn_b, ones)


def _rope_tables(positions, head_dim, lanes):
    rd = head_dim // ROPE_FRACTION
    half = rd // 2
    inv_freq = ROPE_THETA ** (-jnp.arange(half, dtype=F32) / half)
    ang = positions.astype(F32).reshape(-1, 1) * inv_freq
    cos, sin = jnp.cos(ang), jnp.sin(ang)
    t = ang.shape[0]
    pad = jnp.zeros((t, head_dim - rd), F32)
    zero = jnp.zeros((t, half), F32)
    c = jnp.concatenate([cos, cos, pad + 1.0], -1)
    sn = jnp.concatenate([-sin, zero, pad], -1)
    sp = jnp.concatenate([zero, sin, pad], -1)
    rep = lanes // head_dim
    return tuple(jnp.tile(x, (1, rep)) for x in (c, sn, sp)), half


def _rope(x, c, sn, sp, half):
    width = x.shape[1]
    rep = width // c.shape[1]
    if rep > 1:
        c, sn, sp = (jnp.concatenate([y] * rep, axis=1) for y in (c, sn, sp))
    nxt = pltpu.roll(x, width - half, axis=1)
    prv = pltpu.roll(x, half, axis=1)
    return x * c + nxt * sn + prv * sp


def _swa_prep_kernel(q_ref, kv_ref, c_ref, sn_ref, sp_ref, q_out, k_out, v_out, *, half, scale):
    c, sn, sp = c_ref[...], sn_ref[...], sp_ref[...]
    q_out[...] = (_rope(q_ref[...], c, sn, sp, half) * scale).astype(q_out.dtype)
    kv = kv_ref[...]
    kw = kv.shape[1] // 2
    k_out[...] = _rope(kv[:, :kw], c, sn, sp, half).astype(k_out.dtype)
    v_out[...] = kv[:, kw:].astype(v_out.dtype)


def _swa_prep(hb, tabs, half, qw, kvw, tm):
    t = hb.shape[0]
    kern = functools.partial(_swa_prep_kernel, half=half, scale=HEAD_DIM ** -0.5)
    tab_spec = pl.BlockSpec((tm, 128), lambda i: (i, 0))
    return pl.pallas_call(
        kern,
        grid=(t // tm,),
        in_specs=[pl.BlockSpec((tm, qw), lambda i: (i, 0)),
                  pl.BlockSpec((tm, 2 * kvw), lambda i: (i, qw // (2 * kvw))),
                  tab_spec, tab_spec, tab_spec],
        out_specs=[pl.BlockSpec((tm, qw), lambda i: (i, 0)),
                   pl.BlockSpec((tm, kvw), lambda i: (i, 0)),
                   pl.BlockSpec((tm, kvw), lambda i: (i, 0))],
        out_shape=[jax.ShapeDtypeStruct((t, qw), BF16),
                   jax.ShapeDtypeStruct((t, kvw), BF16),
                   jax.ShapeDtypeStruct((t, kvw), BF16)],
        compiler_params=_params("parallel"),
        name="swa_prep",
    )(hb, hb, *tabs)


def _swa_kernel(sink_ref, q_ref, kp_ref, kc_ref, vp_ref, vc_ref, o_ref, *, q_heads, kv_heads, n):
    blk = q_ref.shape[0]
    j = pl.program_id(1)
    kw = jnp.concatenate([kp_ref[...], kc_ref[...]], axis=0)
    vw = jnp.concatenate([vp_ref[...], vc_ref[...]], axis=0)
    qi = lax.broadcasted_iota(jnp.int32, (blk, 2 * blk), 0)
    kj = lax.broadcasted_iota(jnp.int32, (blk, 2 * blk), 1)
    rel = qi + blk - kj
    mask = (rel >= 0) & (rel < WINDOW) & ((kj >= blk) | (j > 0))
    group = q_heads // kv_heads
    for h in range(q_heads):
        g = h // group
        qh = q_ref[:, h * n:(h + 1) * n]
        s = _dot_nt(qh, kw[:, g * n:(g + 1) * n])
        s = jnp.where(mask, s, -jnp.inf)
        sink = sink_ref[h]
        m = jnp.maximum(jnp.max(s, -1, keepdims=True), sink)
        e = jnp.exp(s - m)
        den = jnp.sum(e, -1, keepdims=True) + jnp.exp(sink - m)
        p = (e / den).astype(vw.dtype)
        o_ref[:, h * n:(h + 1) * n] = _dot(p, vw[:, g * n:(g + 1) * n]).astype(o_ref.dtype)


def _swa(q, k, v, sinks, batch, seq):
    t, qw = q.shape
    kvw = k.shape[1]
    n = HEAD_DIM
    blk = WINDOW
    nb = seq // blk
    cur = lambda b, j: (b * nb + j, 0)
    prev = lambda b, j: (b * nb + jnp.maximum(j - 1, 0), 0)
    kern = functools.partial(_swa_kernel, q_heads=qw // n, kv_heads=kvw // n, n=n)
    return pl.pallas_call(
        kern,
        grid=(batch, nb),
        in_specs=[pl.BlockSpec(memory_space=pltpu.SMEM),
                  pl.BlockSpec((blk, qw), cur),
                  pl.BlockSpec((blk, kvw), prev), pl.BlockSpec((blk, kvw), cur),
                  pl.BlockSpec((blk, kvw), prev), pl.BlockSpec((blk, kvw), cur)],
        out_specs=pl.BlockSpec((blk, qw), cur),
        out_shape=jax.ShapeDtypeStruct((t, qw), F32),
        compiler_params=_params("parallel", "arbitrary"),
        name="swa",
    )(sinks, q, k, k, v, v)


def _diff_prep_kernel(q_ref, k_ref, v_ref, c_ref, sn_ref, sp_ref, q_out, k_out, v_out, *, half, scale):
    c, sn, sp = c_ref[...], sn_ref[...], sp_ref[...]
    q_out[...] = (_rope(q_ref[...], c, sn, sp, half) * scale).astype(q_out.dtype)
    k_out[...] = _rope(k_ref[...], c, sn, sp, half).astype(k_out.dtype)
    v_out[...] = v_ref[...].astype(v_out.dtype)


def _diff_prep(h, tabs, half, width, tm):
    t = h.shape[0]
    kern = functools.partial(_diff_prep_kernel, half=half,
                             scale=DIFF_HEAD_DIM ** -0.5 * math.log2(math.e))
    tab_spec = pl.BlockSpec((tm, 128), lambda i: (i, 0))
    col = lambda c: pl.BlockSpec((tm, width), lambda i, c=c: (i, c))
    return pl.pallas_call(
        kern,
        grid=(t // tm,),
        in_specs=[col(0), col(1), col(2), tab_spec, tab_spec, tab_spec],
        out_specs=[col(0)] * 3,
        out_shape=[jax.ShapeDtypeStruct((t, width), BF16)] * 3,
        compiler_params=_params("parallel"),
        name="diff_prep",
    )(h, h, h, *tabs)


def _diff_attn_kernel(sc_ref, q_ref, k_ref, v_ref, g_ref, o_ref, m_ref, l_ref, acc_ref, *, blk, dh):
    qi = pl.program_id(2)
    m_ref[...] = jnp.full_like(m_ref, -jnp.inf)
    l_ref[...] = jnp.zeros_like(l_ref)
    acc_ref[...] = jnp.zeros_like(acc_ref)
    reps = blk // 128

    def block(j, diagonal):
        off = pl.multiple_of(j * blk, blk)
        v = v_ref[pl.ds(off, blk), :]
        ss = [_dot_nt(q_ref[:, c * dh:(c + 1) * dh], k_ref[pl.ds(off, blk), c * dh:(c + 1) * dh])
              for c in range(2)]
        if diagonal:
            row = lax.broadcasted_iota(jnp.int32, (blk, blk), 0)
            col = lax.broadcasted_iota(jnp.int32, (blk, blk), 1)
            ss = [jnp.where(col <= row, s, -jnp.inf) for s in ss]
        ps, alphas = [], []
        for c in range(2):
            m_old = m_ref[c]
            m_new = jnp.maximum(m_old, jnp.max(ss[c], -1, keepdims=True))
            alpha = jnp.exp2(m_old - m_new)
            p = jnp.exp2(ss[c] - jnp.concatenate([m_new] * reps, axis=1))
            l_ref[c] = alpha * l_ref[c] + jnp.sum(p, -1, keepdims=True)
            m_ref[c] = m_new
            ps.append(p.astype(v.dtype))
            alphas.append(jnp.concatenate([alpha] * (2 * dh // 128), axis=1))
        pv = [_dot(p, v) for p in ps]
        for c in range(2):
            acc_ref[c] = alphas[c] * acc_ref[c] + pv[c]

    def body(j, carry):
        block(j, False)
        return carry

    lax.fori_loop(0, qi, body, 0)
    block(qi, True)

    lam = sc_ref[0]
    post = sc_ref[1]
    rep = 2 * dh // 128
    o1 = acc_ref[0] / jnp.concatenate([l_ref[0]] * rep, axis=1)
    o2 = acc_ref[1] / jnp.concatenate([l_ref[1]] * rep, axis=1)
    out = o1 - lam * o2
    out = out * lax.rsqrt(jnp.mean(out * out, -1, keepdims=True) + SUBLN_EPS)
    o_ref[...] = (out * g_ref[...] * post).astype(o_ref.dtype)


def _diff_attn(q, k, v, scalars, subln_g, batch, seq, blk):
    t, width = q.shape
    dh = DIFF_HEAD_DIM
    hw = 2 * dh
    heads = width // hw
    nq = seq // blk
    kern = functools.partial(_diff_attn_kernel, blk=blk, dh=dh)
    resident = pl.BlockSpec((seq, hw), lambda b, h, qi: (b, h), pipeline_mode=pl.Buffered(1))
    return pl.pallas_call(
        kern,
        grid=(batch, heads, nq),
        in_specs=[pl.BlockSpec(memory_space=pltpu.SMEM),
                  pl.BlockSpec((blk, hw), lambda b, h, qi: (b * nq + qi, h)),
                  resident, resident,
                  pl.BlockSpec((1, hw), lambda b, h, qi: (0, 0))],
        out_specs=pl.BlockSpec((blk, hw), lambda b, h, qi: (b * nq + qi, h)),
        out_shape=jax.ShapeDtypeStruct((t, width), BF16),
        scratch_shapes=[pltpu.VMEM((2, blk, 128), F32), pltpu.VMEM((2, blk, 128), F32),
                        pltpu.VMEM((2, blk, hw), F32)],
        compiler_params=_params("parallel", "parallel", "arbitrary"),
        name="diff_attn",
    )(scalars, q, k, v, subln_g)


def _outproj_ln_kernel(*refs, n_in, alpha):
    ys = refs[:n_in]
    ws = refs[n_in:2 * n_in]
    x_ref, g_ref, b_ref, o_ref, obf_ref = refs[2 * n_in:]
    mix = _dot(ys[0][...].astype(BF16), ws[0][...])
    for y, w in zip(ys[1:], ws[1:]):
        mix = mix + _dot(y[...].astype(BF16), w[...])
    out = _layer_norm(alpha * x_ref[...] + mix, g_ref[...], b_ref[...])
    o_ref[...] = out
    obf_ref[...] = out.astype(BF16)


def _outproj_ln(ys, ws, x, g, b, alpha, tm):
    t, d = x.shape
    n_in = len(ys)
    kern = functools.partial(_outproj_ln_kernel, n_in=n_in, alpha=alpha)
    row = lambda width: pl.BlockSpec((tm, width), lambda i: (i, 0))
    full = lambda arr: pl.BlockSpec(arr.shape, lambda i: (0, 0))
    return pl.pallas_call(
        kern,
        grid=(t // tm,),
        in_specs=[row(y.shape[1]) for y in ys] + [full(w) for w in ws] + [row(d), full(g), full(b)],
        out_specs=[row(d), row(d)],
        out_shape=[jax.ShapeDtypeStruct((t, d), F32), jax.ShapeDtypeStruct((t, d), BF16)],
        compiler_params=_params("parallel"),
        name="outproj_ln",
    )(*ys, *ws, x, g, b)


def _ffn_up_kernel(x_ref, w1_ref, w3_ref, o_ref):
    x = x_ref[...]
    h1 = _dot(x, w1_ref[...])
    h3 = _dot(x, w3_ref[...])
    o_ref[...] = (h1 * jax.nn.sigmoid(h1) * h3).astype(o_ref.dtype)


def _ffn_up(x, w1, w3, tm, tf):
    t, d = x.shape
    f = w1.shape[1]
    return pl.pallas_call(
        _ffn_up_kernel,
        grid=(t // tm, f // tf),
        in_specs=[pl.BlockSpec((tm, d), lambda i, j: (i, 0)),
                  pl.BlockSpec((d, tf), lambda i, j: (0, j)),
                  pl.BlockSpec((d, tf), lambda i, j: (0, j))],
        out_specs=pl.BlockSpec((tm, tf), lambda i, j: (i, j)),
        out_shape=jax.ShapeDtypeStruct((t, f), BF16),
        compiler_params=_params("parallel", "arbitrary"),
        name="ffn_up",
    )(x, w1, w3)


def _ffn_down_ln_kernel(a_ref, w_ref, x_ref, g_ref, b_ref, o_ref, obf_ref, acc_ref, *, alpha):
    kstep = pl.program_id(1)

    @pl.when(kstep == 0)
    def _():
        acc_ref[...] = jnp.zeros_like(acc_ref)

    acc_ref[...] += _dot(a_ref[...], w_ref[...])

    @pl.when(kstep == pl.num_programs(1) - 1)
    def _():
        out = _layer_norm(alpha * x_ref[...] + acc_ref[...], g_ref[...], b_ref[...])
        o_ref[...] = out
        obf_ref[...] = out.astype(BF16)


def _ffn_down_ln(act, w2, x, g, b, alpha, tm, tk):
    t, d = x.shape
    f = act.shape[1]
    kern = functools.partial(_ffn_down_ln_kernel, alpha=alpha)
    row = pl.BlockSpec((tm, d), lambda i, j: (i, 0))
    par = pl.BlockSpec((1, d), lambda i, j: (0, 0))
    return pl.pallas_call(
        kern,
        grid=(t // tm, f // tk),
        in_specs=[pl.BlockSpec((tm, tk), lambda i, j: (i, j)),
                  pl.BlockSpec((tk, d), lambda i, j: (j, 0)),
                  row, par, par],
        out_specs=[row, row],
        out_shape=[jax.ShapeDtypeStruct((t, d), F32), jax.ShapeDtypeStruct((t, d), BF16)],
        scratch_shapes=[pltpu.VMEM((tm, d), F32)],
        compiler_params=_params("parallel", "arbitrary"),
        name="ffn_down_ln",
    )(act, w2, x, g, b)


def _ple_kernel(xbf_ref, wg_ref, p_ref, wp_ref, x_ref, o_ref, obf_ref):
    gate = jax.nn.sigmoid(_dot(xbf_ref[...], wg_ref[...]))
    proj = _dot(p_ref[...].astype(BF16), wp_ref[...])
    out = x_ref[...] + gate * proj
    o_ref[...] = out
    obf_ref[...] = out.astype(BF16)


def _ple(xbf, wg, p, wp, x, tm, tn):
    t, d = x.shape
    pd = p.shape[1]
    out = pl.BlockSpec((tm, tn), lambda i, j: (i, j))
    return pl.pallas_call(
        _ple_kernel,
        grid=(t // tm, d // tn),
        in_specs=[pl.BlockSpec((tm, d), lambda i, j: (i, 0)),
                  pl.BlockSpec((d, tn), lambda i, j: (0, j)),
                  pl.BlockSpec((tm, pd), lambda i, j: (i, 0)),
                  pl.BlockSpec((pd, tn), lambda i, j: (0, j)),
                  out],
        out_specs=[out, out],
        out_shape=[jax.ShapeDtypeStruct((t, d), F32), jax.ShapeDtypeStruct((t, d), BF16)],
        compiler_params=_params("parallel", "arbitrary"),
        name="ple",
    )(xbf, wg, p, wp, x)


def _even_mixer(xbf, positions, w_in, mu, w0, w_up, a0, a_up, g_up, k_k, k_a, r_k, gn_w, gn_b,
                sinks, batch, seq):
    aw = w0.shape[0]
    dl = w_up.shape[0]
    al = a_up.shape[0]
    gl = g_up.shape[0]
    a_cols = 3 * aw + dl + al + gl
    gl_pad = 384
    assert dl == 64 and al == 64 and gl <= gl_pad
    d = w_in.shape[0]
    pad_cols = jnp.zeros((d, gl_pad - gl), F32)
    w_a = jnp.concatenate([w_in[:, :a_cols], pad_cols], axis=1).astype(BF16)
    w_b = w_in[:, a_cols:].astype(BF16)
    mu_p = jnp.concatenate([mu, jnp.zeros((gl_pad - gl,), F32)])[None, :]
    zeros = jnp.zeros((dl, aw), F32)
    wwa = jnp.concatenate([jnp.concatenate([w_up, zeros], 1), jnp.concatenate([zeros, a_up], 1)], 0)
    gup = jnp.concatenate([g_up, jnp.zeros((gl_pad - gl, aw), F32)], 0)

    h_a = _matmul(xbf, w_a, F32, 1024, 512, "proj_rwkv")
    h_b = _matmul(xbf, w_b, F32, 1024, w_b.shape[1], "proj_swa")

    ones = jnp.kron(jnp.eye(2, dtype=F32), jnp.ones((HEAD_DIM, HEAD_DIM), F32)).astype(BF16)
    at, bt, kt, rt, bh, kh, v, g, bvg, wc = _rwkv_prep(
        h_a, mu_p, w0[None], a0[None], k_k[None], k_a[None], r_k.reshape(1, aw), wwa, gup, ones,
        seq, 256)
    y = _rwkv_scan(at, bt, kt, rt, bh, kh, v, wc, batch, seq)
    ya = _rwkv_post(y, g, bvg, gn_w[None], gn_b[None], ones, 512)

    qw = sinks.shape[0] * HEAD_DIM
    kvw = (w_b.shape[1] - qw) // 2
    tabs, half = _rope_tables(positions, HEAD_DIM, 128)
    q, kb, vb = _swa_prep(h_b, tabs, half, qw, kvw, 512)
    yb = _swa(q, kb, vb, sinks, batch, seq)
    return ya, yb


def _odd_mixer(xbf, positions, w_in, lam_p, subln_g, layer_idx, batch, seq):
    width = w_in.shape[1] // 3
    h = _matmul(xbf, w_in.astype(BF16), F32, 1024, 1024, "proj_diff")
    tabs, half = _rope_tables(positions, DIFF_HEAD_DIM, 128)
    q, k, v = _diff_prep(h, tabs, half, width, 256)
    lam_init = 0.8 - 0.6 * math.exp(-0.3 * layer_idx)
    lv = lam_p.astype(F32)
    lam = jnp.exp(jnp.sum(lv[0] * lv[1])) - jnp.exp(jnp.sum(lv[2] * lv[3])) + lam_init
    scalars = jnp.stack([lam, jnp.asarray(1.0 - lam_init, F32)]).astype(F32)
    return _diff_attn(q, k, v, scalars, subln_g[None], batch, seq, 512)


def kernel(x, p, positions, e_w_in, e_mu, e_w0, e_w_up, e_a0, e_a_up, e_g_up, e_k_k, e_k_a, e_r_k,
           e_gn_w, e_gn_b, e_sinks, e_w_out, o_w_in, o_lambda, o_subln_g, o_w_out,
           ln1_g, ln1_b, ln2_g, ln2_b, ffn_w1, ffn_w3, ffn_w2, ple_w_proj, ple_w_gate):
    batch, seq, d = x.shape
    depth = p.shape[0]
    t = batch * seq
    alpha = (2.0 * depth) ** 0.25
    xf = x.reshape(t, d)
    xbf = xf.astype(BF16)
    for i in range(depth):
        j = i // 2
        if i % 2 == 0:
            ya, yb = _even_mixer(xbf, positions, e_w_in[j], e_mu[j], e_w0[j], e_w_up[j], e_a0[j],
                                 e_a_up[j], e_g_up[j], e_k_k[j], e_k_a[j], e_r_k[j], e_gn_w[j],
                                 e_gn_b[j], e_sinks[j], batch, seq)
            aw = ya.shape[1]
            w_out = e_w_out[j].astype(BF16)
            ys, ws = [ya, yb], [w_out[:aw], w_out[aw:]]
        else:
            y = _odd_mixer(xbf, positions, o_w_in[j], o_lambda[j], o_subln_g[j], i, batch, seq)
            ys, ws = [y], [o_w_out[j].astype(BF16)]
        xf, xbf = _outproj_ln(ys, ws, xf, ln1_g[i][None], ln1_b[i][None], alpha, 256)
        act = _ffn_up(xbf, ffn_w1[i].astype(BF16), ffn_w3[i].astype(BF16), 1024, 512)
        xf, xbf = _ffn_down_ln(act, ffn_w2[i].astype(BF16), xf, ln2_g[i][None], ln2_b[i][None],
                               alpha, 512, 1408)
        xf, xbf = _ple(xbf, ple_w_gate[i].astype(BF16), p[i].reshape(t, -1),
                       ple_w_proj[i].astype(BF16), xf, 512, 1024)
    return xf.reshape(batch, seq, d)
```

```python
import functools
import math

import jax
import jax.numpy as jnp
from jax import lax
from jax.experimental import pallas as pl
from jax.experimental.pallas import tpu as pltpu

F32 = jnp.float32
BF16 = jnp.bfloat16
HI = lax.Precision.HIGHEST

HEAD_DIM = 64
DIFF_HEAD_DIM = 128
WINDOW = 128
ROPE_THETA = 500000.0
ROPE_FRACTION = 4
LN_EPS = 1e-5
WKV_GN_EPS = 64e-5
SUBLN_EPS = 1e-5
RWKV_CHUNK = 64
VMEM_LIMIT = 56 * 1024 * 1024


def _dot(a, b, prec=None):
    return jnp.dot(a, b, preferred_element_type=F32, precision=prec)


def _dot_nt(a, b, prec=None):
    return lax.dot_general(a, b, (((1,), (1,)), ((), ())), preferred_element_type=F32, precision=prec)


def _dot_tn(a, b, prec=None):
    return lax.dot_general(a, b, (((0,), (0,)), ((), ())), preferred_element_type=F32, precision=prec)


def _params(*sem):
    return pltpu.CompilerParams(dimension_semantics=sem, vmem_limit_bytes=VMEM_LIMIT)


def _layer_norm(z, g, b):
    mu = jnp.mean(z, -1, keepdims=True)
    d = z - mu
    var = jnp.mean(d * d, -1, keepdims=True)
    return d * lax.rsqrt(var + LN_EPS) * g + b


def _mm_kernel(x_ref, w_ref, o_ref):
    o_ref[...] = _dot(x_ref[...], w_ref[...]).astype(o_ref.dtype)


def _matmul(x, w, out_dtype, tm, tn, name):
    t, k = x.shape
    n = w.shape[1]
    return pl.pallas_call(
        _mm_kernel,
        grid=(t // tm, n // tn),
        in_specs=[pl.BlockSpec((tm, k), lambda i, j: (i, 0)),
                  pl.BlockSpec((k, tn), lambda i, j: (0, j))],
        out_specs=pl.BlockSpec((tm, tn), lambda i, j: (i, j)),
        out_shape=jax.ShapeDtypeStruct((t, n), out_dtype),
        compiler_params=_params("parallel", "arbitrary"),
        name=name,
    )(x, w)


def _bf16_parts(x, n):
    parts = []
    for _ in range(n):
        p = x.astype(BF16)
        parts.append(p)
        x = x - p.astype(F32)
    return parts


def _split_dot(x, w, n=2):
    return sum(_dot(p, w) for p in _bf16_parts(x, n))


def _dot3(a, b):
    ah, al = _bf16_parts(a, 2)
    bh, bl = _bf16_parts(b, 2)
    return _dot(ah, bh) + _dot(ah, bl) + _dot(al, bh)


def _head_sums(x, ones):
    gw = ones.shape[0]
    parts = [_split_dot(x[:, i:i + gw], ones) for i in range(0, x.shape[1], gw)]
    return jnp.concatenate(parts, axis=1)


def _rwkv_prep_kernel(h_ref, hp_ref, mu_ref, w0_ref, a0_ref, kk_ref, ka_ref, rk_ref, wwa_ref,
                      gup_ref, ones_ref, at_out, bt_out, kt_out, rt_out, bh_out, kh_out, v_out,
                      g_out, bvg_out, wc_out, *, blocks_per_seq, aw, chunk):
    i = pl.program_id(0)
    h = h_ref[...]
    tm = h.shape[0]
    first = (i % blocks_per_seq) == 0
    prev_row = jnp.where(first, 0.0, hp_ref[7:8, :])
    shifted = pltpu.roll(h, 1, axis=0)
    row = lax.broadcasted_iota(jnp.int32, h.shape, 0)
    shifted = jnp.where(row == 0, prev_row, shifted)
    hm = h + (shifted - h) * mu_ref[...]
    r = hm[:, 0:aw]
    k = hm[:, aw:2 * aw]
    v = hm[:, 2 * aw:3 * aw]
    z = hm[:, 3 * aw:3 * aw + 128]
    lane = lax.broadcasted_iota(jnp.int32, z.shape, 1)
    z = jnp.where(lane < 64, jnp.tanh(z), z)
    wa = _dot3(z, wwa_ref[...])
    w = -jax.nn.softplus(-(w0_ref[...] + wa[:, :aw])) - 0.5
    a = jax.nn.sigmoid(a0_ref[...] + wa[:, aw:])
    lg = hm[:, 3 * aw + 128:]
    g = _dot3(jax.nn.sigmoid(lg), gup_ref[...])
    lw = -jnp.exp(w)

    ri = lax.broadcasted_iota(jnp.int32, (tm, tm), 0)
    ci = lax.broadcasted_iota(jnp.int32, (tm, tm), 1)
    shift = int(math.log2(chunk))
    same = (ri >> shift) == (ci >> shift)
    lw_parts = _bf16_parts(lw, 3)
    tri = jnp.where(same & (ri >= ci), 1.0, 0.0).astype(BF16)
    blk = jnp.where(same, 1.0, 0.0).astype(BF16)
    cs = sum(_dot(tri, part) for part in lw_parts)
    tot = sum(_dot(blk, part) for part in lw_parts)

    ones = ones_ref[...]
    kkr = k * kk_ref[...]
    kk = kkr / jnp.maximum(jnp.sqrt(_head_sums(kkr * kkr, ones)), 1e-12)
    k2 = k * (1.0 + (a - 1.0) * ka_ref[...])
    bb = kk * a
    w_inv = jnp.exp(-cs)
    w_rem = jnp.exp(tot - cs)
    at_out[...] = (-kk * jnp.exp(cs - lw)).astype(BF16)
    bt_out[...] = (bb * w_inv).astype(BF16)
    kt_out[...] = (k2 * w_inv).astype(BF16)
    rt_out[...] = (r * jnp.exp(cs)).astype(BF16)
    bh_out[...] = (bb * w_rem).astype(BF16)
    kh_out[...] = (k2 * w_rem).astype(BF16)
    v_out[...] = v.astype(BF16)
    g_out[...] = g
    bvg_out[...] = _head_sums(r * k2 * rk_ref[...], ones) * v * g
    for c in range(tm // chunk):
        wc_out[c] = jnp.exp(tot[c * chunk:c * chunk + 1, :])


def _rwkv_prep(h, mu, w0, a0, k_k, k_a, r_k, wwa, gup, ones, seq, tm):
    t, cols = h.shape
    aw = w0.shape[1]
    chunk = RWKV_CHUNK
    row_spec = pl.BlockSpec((tm, aw), lambda i: (i, 0))
    full = lambda arr: pl.BlockSpec(arr.shape, lambda i: (0, 0))
    kern = functools.partial(_rwkv_prep_kernel, blocks_per_seq=seq // tm, aw=aw, chunk=chunk)
    return pl.pallas_call(
        kern,
        grid=(t // tm,),
        in_specs=[pl.BlockSpec((tm, cols), lambda i: (i, 0)),
                  pl.BlockSpec((8, cols), lambda i: (jnp.maximum(i * (tm // 8) - 1, 0), 0)),
                  full(mu), full(w0), full(a0), full(k_k), full(k_a), full(r_k), full(wwa),
                  full(gup), full(ones)],
        out_specs=[row_spec] * 9 + [pl.BlockSpec((tm // chunk, 1, aw), lambda i: (i, 0, 0))],
        out_shape=([jax.ShapeDtypeStruct((t, aw), BF16)] * 7 + [jax.ShapeDtypeStruct((t, aw), F32)] * 2
                   + [jax.ShapeDtypeStruct((t // chunk, 1, aw), F32)]),
        compiler_params=_params("parallel"),
        name="rwkv_prep",
    )(h, h, mu, w0, a0, k_k, k_a, r_k, wwa, gup, ones)


def _rwkv_scan_kernel(at_ref, bt_ref, kt_ref, rt_ref, bh_ref, kh_ref, v_ref, wc_ref, y_ref, st_ref,
                      *, pairs, n):
    c = at_ref.shape[0]
    n2 = 2 * n

    @pl.when(pl.program_id(1) == 0)
    def _():
        st_ref[...] = jnp.zeros_like(st_ref)

    row = lax.broadcasted_iota(jnp.int32, (n2, n2), 0)
    col = lax.broadcasted_iota(jnp.int32, (n2, n2), 1)
    same = (row >= c) == (col >= c)
    strict = same & (row > col)
    incl = same & (row >= col)
    incl2 = jnp.concatenate([incl, incl], axis=1)
    eye = (row == col).astype(F32)
    lo = lax.broadcasted_iota(jnp.int32, (c, n2), 1) < n

    def stack(ref, sl):
        x = ref[:, sl]
        zero = jnp.zeros_like(x)
        return jnp.concatenate([jnp.where(lo, x, zero), jnp.where(lo, zero, x)], axis=0)

    sls = [slice(p * n2, (p + 1) * n2) for p in range(pairs)]
    stacks = [[stack(ref, sl) for ref in (at_ref, bt_ref, kt_ref, rt_ref, bh_ref, kh_ref, v_ref)]
              for sl in sls]
    ar = [jnp.concatenate([st[0], st[3]], axis=0) for st in stacks]
    big = [_dot_nt(ar[p], jnp.concatenate([stacks[p][1], stacks[p][2]], axis=0)) for p in range(pairs)]
    l_ab = [jnp.where(strict, b[:n2, :n2], 0.0) for b in big]
    l_ak = [jnp.where(strict, b[:n2, n2:], 0.0).astype(BF16) for b in big]
    m_all = [jnp.where(incl2, b[n2:, :], 0.0).astype(BF16) for b in big]

    tinv = [eye + l for l in l_ab]
    pw = [l.astype(BF16) for l in l_ab]
    for _ in range(int(math.log2(c)) - 1):
        pw = [_dot(x, x).astype(BF16) for x in pw]
        tinv = [t + _dot(t.astype(BF16), x) for t, x in zip(tinv, pw)]
    tinv = [t.astype(BF16) for t in tinv]

    s0 = [st_ref[p] for p in range(pairs)]
    from_state = [_dot_nt(ar[p], s0[p].astype(BF16)) for p in range(pairs)]
    x1 = [(from_state[p][:n2] + _dot(l_ak[p], stacks[p][6])).astype(BF16) for p in range(pairs)]
    u = [_dot(tinv[p], x1[p]).astype(BF16) for p in range(pairs)]
    uv = [jnp.concatenate([u[p], stacks[p][6]], axis=0) for p in range(pairs)]
    for p in range(pairs):
        yst = from_state[p][n2:] + _dot(m_all[p], uv[p])
        st_ref[p] = s0[p] * wc_ref[0, :, sls[p]] + _dot_tn(
            uv[p], jnp.concatenate([stacks[p][4], stacks[p][5]], axis=0))
        y_ref[:, sls[p]] = yst[:c] + yst[c:]


def _rwkv_scan(at, bt, kt, rt, bh, kh, v, wc, batch, seq):
    t, aw = at.shape
    c = RWKV_CHUNK
    n = HEAD_DIM
    assert c == n
    pairs = aw // (2 * n)
    nchunk = seq // c
    tok = pl.BlockSpec((c, aw), lambda b, j: (b * nchunk + j, 0))
    kern = functools.partial(_rwkv_scan_kernel, pairs=pairs, n=n)
    return pl.pallas_call(
        kern,
        grid=(batch, nchunk),
        in_specs=[tok] * 7 + [pl.BlockSpec((1, 1, aw), lambda b, j: (b * nchunk + j, 0, 0))],
        out_specs=tok,
        out_shape=jax.ShapeDtypeStruct((t, aw), F32),
        scratch_shapes=[pltpu.VMEM((pairs, 2 * n, 2 * n), F32)],
        compiler_params=_params("parallel", "arbitrary"),
        name="rwkv_scan",
    )(at, bt, kt, rt, bh, kh, v, wc)


def _rwkv_post_kernel(y_ref, g_ref, bvg_ref, gnw_ref, gnb_ref, ones_ref, o_ref, *, n):
    y = y_ref[...]
    ones = ones_ref[...]
    ym = _head_sums(y, ones) * (1.0 / n)
    d = y - ym
    yv = _head_sums(d * d, ones) * (1.0 / n)
    yn = d * lax.rsqrt(yv + WKV_GN_EPS) * gnw_ref[...] + gnb_ref[...]
    o_ref[...] = (yn * g_ref[...] + bvg_ref[...]).astype(o_ref.dtype)


def _rwkv_post(y, g, bvg, gn_w, gn_b, ones, tm):
    t, aw = y.shape
    row = pl.BlockSpec((tm, aw), lambda i: (i, 0))
    full = lambda arr: pl.BlockSpec(arr.shape, lambda i: (0, 0))
    return pl.pallas_call(
        functools.partial(_rwkv_post_kernel, n=HEAD_DIM),
        grid=(t // tm,),
        in_specs=[row, row, row, full(gn_w), full(gn_b), full(ones)],
        out_specs=row,
        out_shape=jax.ShapeDtypeStruct((t, aw), BF16),
        compiler_params=_params("parallel"),
        name="rwkv_post",
    )(y, g, bvg, gn_w, gn_b, ones)


def _rope_tables(positions, head_dim, lanes):
    rd = head_dim // ROPE_FRACTION
    half = rd // 2
    inv_freq = ROPE_THETA ** (-jnp.arange(half, dtype=F32) / half)
    ang = positions.astype(F32).reshape(-1, 1) * inv_freq
    cos, sin = jnp.cos(ang), jnp.sin(ang)
    t = ang.shape[0]
    pad = jnp.zeros((t, head_dim - rd), F32)
    zero = jnp.zeros((t, half), F32)
    c = jnp.concatenate([cos, cos, pad + 1.0], -1)
    sn = jnp.concatenate([-sin, zero, pad], -1)
    sp = jnp.concatenate([zero, sin, pad], -1)
    rep = lanes // head_dim
    return tuple(jnp.tile(x, (1, rep)) for x in (c, sn, sp)), half


def _rope(x, c, sn, sp, half):
    width = x.shape[1]
    rep = width // c.shape[1]
    if rep > 1:
        c, sn, sp = (jnp.concatenate([y] * rep, axis=1) for y in (c, sn, sp))
    nxt = pltpu.roll(x, width - half, axis=1)
    prv = pltpu.roll(x, half, axis=1)
    return x * c + nxt * sn + prv * sp


def _swa_prep_kernel(q_ref, kv_ref, c_ref, sn_ref, sp_ref, q_out, k_out, v_out, *, half, scale):
    c, sn, sp = c_ref[...], sn_ref[...], sp_ref[...]
    q_out[...] = (_rope(q_ref[...], c, sn, sp, half) * scale).astype(q_out.dtype)
    kv = kv_ref[...]
    kw = kv.shape[1] // 2
    k_out[...] = _rope(kv[:, :kw], c, sn, sp, half).astype(k_out.dtype)
    v_out[...] = kv[:, kw:].astype(v_out.dtype)


def _swa_prep(hb, tabs, half, qw, kvw, tm):
    t = hb.shape[0]
    kern = functools.partial(_swa_prep_kernel, half=half, scale=HEAD_DIM ** -0.5)
    tab_spec = pl.BlockSpec((tm, 128), lambda i: (i, 0))
    return pl.pallas_call(
        kern,
        grid=(t // tm,),
        in_specs=[pl.BlockSpec((tm, qw), lambda i: (i, 0)),
                  pl.BlockSpec((tm, 2 * kvw), lambda i: (i, qw // (2 * kvw))),
                  tab_spec, tab_spec, tab_spec],
        out_specs=[pl.BlockSpec((tm, qw), lambda i: (i, 0)),
                   pl.BlockSpec((tm, kvw), lambda i: (i, 0)),
                   pl.BlockSpec((tm, kvw), lambda i: (i, 0))],
        out_shape=[jax.ShapeDtypeStruct((t, qw), BF16),
                   jax.ShapeDtypeStruct((t, kvw), BF16),
                   jax.ShapeDtypeStruct((t, kvw), BF16)],
        compiler_params=_params("parallel"),
        name="swa_prep",
    )(hb, hb, *tabs)


def _swa_kernel(sink_ref, q_ref, kp_ref, kc_ref, vp_ref, vc_ref, o_ref, *, q_heads, kv_heads, n):
    blk = q_ref.shape[0]
    j = pl.program_id(1)
    kw = jnp.concatenate([kp_ref[...], kc_ref[...]], axis=0)
    vw = jnp.concatenate([vp_ref[...], vc_ref[...]], axis=0)
    qi = lax.broadcasted_iota(jnp.int32, (blk, 2 * blk), 0)
    kj = lax.broadcasted_iota(jnp.int32, (blk, 2 * blk), 1)
    rel = qi + blk - kj
    mask = (rel >= 0) & (rel < WINDOW) & ((kj >= blk) | (j > 0))
    group = q_heads // kv_heads
    ss = [_dot_nt(q_ref[:, h * n:(h + 1) * n], kw[:, (h // group) * n:(h // group + 1) * n])
          for h in range(q_heads)]
    ps = []
    for h in range(q_heads):
        s = jnp.where(mask, ss[h], -jnp.inf)
        sink = sink_ref[h]
        m = jnp.maximum(jnp.max(s, -1, keepdims=True), sink)
        e = jnp.exp(s - m)
        den = jnp.sum(e, -1, keepdims=True) + jnp.exp(sink - m)
        ps.append((e / den).astype(vw.dtype))
    for h in range(q_heads):
        g = h // group
        o_ref[:, h * n:(h + 1) * n] = _dot(ps[h], vw[:, g * n:(g + 1) * n]).astype(o_ref.dtype)


def _swa(q, k, v, sinks, batch, seq):
    t, qw = q.shape
    kvw = k.shape[1]
    n = HEAD_DIM
    blk = WINDOW
    nb = seq // blk
    cur = lambda b, j: (b * nb + j, 0)
    prev = lambda b, j: (b * nb + jnp.maximum(j - 1, 0), 0)
    kern = functools.partial(_swa_kernel, q_heads=qw // n, kv_heads=kvw // n, n=n)
    return pl.pallas_call(
        kern,
        grid=(batch, nb),
        in_specs=[pl.BlockSpec(memory_space=pltpu.SMEM),
                  pl.BlockSpec((blk, qw), cur),
                  pl.BlockSpec((blk, kvw), prev), pl.BlockSpec((blk, kvw), cur),
                  pl.BlockSpec((blk, kvw), prev), pl.BlockSpec((blk, kvw), cur)],
        out_specs=pl.BlockSpec((blk, qw), cur),
        out_shape=jax.ShapeDtypeStruct((t, qw), F32),
        compiler_params=_params("parallel", "arbitrary"),
        name="swa",
    )(sinks, q, k, k, v, v)


def _diff_rope_kernel(x_ref, c_ref, sn_ref, sp_ref, o_ref, *, half):
    x = x_ref[...].astype(F32)
    o_ref[...] = _rope(x, c_ref[...], sn_ref[...], sp_ref[...], half).astype(o_ref.dtype)


def _diff_rope(h, tabs, half, width, tm, tn):
    t = h.shape[0]
    tab_spec = pl.BlockSpec((tm, 128), lambda i, j: (i, 0))
    blk = pl.BlockSpec((tm, tn), lambda i, j: (i, j))
    return pl.pallas_call(
        functools.partial(_diff_rope_kernel, half=half),
        grid=(t // tm, 2 * width // tn),
        in_specs=[blk, tab_spec, tab_spec, tab_spec],
        out_specs=blk,
        out_shape=jax.ShapeDtypeStruct((t, 2 * width), h.dtype),
        compiler_params=_params("parallel", "arbitrary"),
        name="diff_rope",
    )(h, *tabs)


def _diff_attn_kernel(sc_ref, q_ref, k_ref, v_ref, g_ref, o_ref, m_ref, l_ref, acc_ref, *, bq, bk, dh):
    qi = pl.program_id(2)
    m_ref[...] = jnp.full_like(m_ref, -jnp.inf)
    l_ref[...] = jnp.zeros_like(l_ref)
    acc_ref[...] = jnp.zeros_like(acc_ref)
    ratio = bq // bk

    def block(j, diag_offset):
        off = pl.multiple_of(j * bk, bk)
        v = v_ref[pl.ds(off, bk), :]
        ss = [_dot_nt(q_ref[:, c * dh:(c + 1) * dh], k_ref[pl.ds(off, bk), c * dh:(c + 1) * dh])
              for c in range(2)]
        if diag_offset is not None:
            row = lax.broadcasted_iota(jnp.int32, (bq, bk), 0)
            col = lax.broadcasted_iota(jnp.int32, (bq, bk), 1) + diag_offset
            ss = [jnp.where(col <= row, s, -jnp.inf) for s in ss]
        ps, alphas = [], []
        for c in range(2):
            m_old = m_ref[c]
            m_new = jnp.maximum(m_old, jnp.max(ss[c], -1, keepdims=True))
            alpha = jnp.exp2(m_old - m_new)
            p = jnp.exp2(ss[c] - jnp.concatenate([m_new] * (bk // 128), axis=1))
            l_ref[c] = alpha * l_ref[c] + jnp.sum(p, -1, keepdims=True)
            m_ref[c] = m_new
            ps.append(p.astype(v.dtype))
            alphas.append(jnp.concatenate([alpha] * (2 * dh // 128), axis=1))
        pv = [_dot(p, v) for p in ps]
        for c in range(2):
            acc_ref[c] = alphas[c] * acc_ref[c] + pv[c]

    def body(j, carry):
        block(j, None)
        return carry

    lax.fori_loop(0, qi * ratio, body, 0)
    for h in range(ratio):
        block(qi * ratio + h, h * bk)

    lam = sc_ref[0]
    post = sc_ref[1]
    rep = 2 * dh // 128
    o1 = acc_ref[0] / jnp.concatenate([l_ref[0]] * rep, axis=1)
    o2 = acc_ref[1] / jnp.concatenate([l_ref[1]] * rep, axis=1)
    out = o1 - lam * o2
    out = out * lax.rsqrt(jnp.mean(out * out, -1, keepdims=True) + SUBLN_EPS)
    o_ref[...] = (out * g_ref[...] * post).astype(o_ref.dtype)


def _diff_attn(qk, hv, scalars, subln_g, batch, seq, bq, bk):
    t = qk.shape[0]
    width = qk.shape[1] // 2
    dh = DIFF_HEAD_DIM
    hw = 2 * dh
    heads = width // hw
    nq = seq // bq
    kern = functools.partial(_diff_attn_kernel, bq=bq, bk=bk, dh=dh)
    resident = lambda first: pl.BlockSpec((seq, hw), lambda b, h, qi: (b, first + h),
                                          pipeline_mode=pl.Buffered(1))
    return pl.pallas_call(
        kern,
        grid=(batch, heads, nq),
        in_specs=[pl.BlockSpec(memory_space=pltpu.SMEM),
                  pl.BlockSpec((bq, hw), lambda b, h, qi: (b * nq + qi, h)),
                  resident(heads), resident(2 * heads),
                  pl.BlockSpec((1, hw), lambda b, h, qi: (0, 0))],
        out_specs=pl.BlockSpec((bq, hw), lambda b, h, qi: (b * nq + qi, h)),
        out_shape=jax.ShapeDtypeStruct((t, width), BF16),
        scratch_shapes=[pltpu.VMEM((2, bq, 128), F32), pltpu.VMEM((2, bq, 128), F32),
                        pltpu.VMEM((2, bq, hw), F32)],
        compiler_params=_params("parallel", "parallel", "arbitrary"),
        name="diff_attn",
    )(scalars, qk, qk, hv, subln_g)


def _outproj_ln_kernel(*refs, n_in, alpha):
    ys = refs[:n_in]
    ws = refs[n_in:2 * n_in]
    x_ref, g_ref, b_ref, o_ref, obf_ref = refs[2 * n_in:]
    mix = _dot(ys[0][...].astype(BF16), ws[0][...])
    for y, w in zip(ys[1:], ws[1:]):
        mix = mix + _dot(y[...].astype(BF16), w[...])
    out = _layer_norm(alpha * x_ref[...] + mix, g_ref[...], b_ref[...])
    o_ref[...] = out
    obf_ref[...] = out.astype(BF16)


def _outproj_ln(ys, ws, x, g, b, alpha, tm):
    t, d = x.shape
    n_in = len(ys)
    kern = functools.partial(_outproj_ln_kernel, n_in=n_in, alpha=alpha)
    row = lambda width: pl.BlockSpec((tm, width), lambda i: (i, 0))
    full = lambda arr: pl.BlockSpec(arr.shape, lambda i: (0, 0))
    return pl.pallas_call(
        kern,
        grid=(t // tm,),
        in_specs=[row(y.shape[1]) for y in ys] + [full(w) for w in ws] + [row(d), full(g), full(b)],
        out_specs=[row(d), row(d)],
        out_shape=[jax.ShapeDtypeStruct((t, d), F32), jax.ShapeDtypeStruct((t, d), BF16)],
        compiler_params=_params("parallel"),
        name="outproj_ln",
    )(*ys, *ws, x, g, b)


def _ffn_up_kernel(x_ref, w1_ref, w3_ref, o_ref):
    x = x_ref[...]
    h1 = _dot(x, w1_ref[...])
    h3 = _dot(x, w3_ref[...])
    o_ref[...] = (h1 * jax.nn.sigmoid(h1) * h3).astype(o_ref.dtype)


def _ffn_up(x, w1, w3, tm, tf):
    t, d = x.shape
    f = w1.shape[1]
    return pl.pallas_call(
        _ffn_up_kernel,
        grid=(t // tm, f // tf),
        in_specs=[pl.BlockSpec((tm, d), lambda i, j: (i, 0)),
                  pl.BlockSpec((d, tf), lambda i, j: (0, j)),
                  pl.BlockSpec((d, tf), lambda i, j: (0, j))],
        out_specs=pl.BlockSpec((tm, tf), lambda i, j: (i, j)),
        out_shape=jax.ShapeDtypeStruct((t, f), BF16),
        compiler_params=_params("parallel", "arbitrary"),
        name="ffn_up",
    )(x, w1, w3)


def _ffn_down_ple_kernel(a_ref, w2_ref, x_ref, g_ref, b_ref, p_ref, wg_ref, wp_ref, o_ref, obf_ref,
                         *, alpha):
    y = _dot(a_ref[...], w2_ref[...])
    x2 = _layer_norm(alpha * x_ref[...] + y, g_ref[...], b_ref[...])
    gate = jax.nn.sigmoid(_dot(x2.astype(BF16), wg_ref[...]))
    proj = _dot(p_ref[...].astype(BF16), wp_ref[...])
    out = x2 + gate * proj
    o_ref[...] = out
    obf_ref[...] = out.astype(BF16)


def _ffn_down_ple(act, w2, x, g, b, p, wg, wp, alpha, tm):
    t, d = x.shape
    f = act.shape[1]
    pd = p.shape[1]
    kern = functools.partial(_ffn_down_ple_kernel, alpha=alpha)
    row = lambda width: pl.BlockSpec((tm, width), lambda i: (i, 0))
    once = lambda arr: pl.BlockSpec(arr.shape, lambda i: (0, 0), pipeline_mode=pl.Buffered(1))
    return pl.pallas_call(
        kern,
        grid=(t // tm,),
        in_specs=[row(f), once(w2), row(d), once(g), once(b), row(pd), once(wg), once(wp)],
        out_specs=[row(d), row(d)],
        out_shape=[jax.ShapeDtypeStruct((t, d), F32), jax.ShapeDtypeStruct((t, d), BF16)],
        compiler_params=_params("parallel"),
        name="ffn_down_ple",
    )(act, w2, x, g, b, p, wg, wp)


def _even_mixer(xbf, positions, w_in, mu, w0, w_up, a0, a_up, g_up, k_k, k_a, r_k, gn_w, gn_b,
                sinks, batch, seq):
    aw = w0.shape[0]
    dl = w_up.shape[0]
    al = a_up.shape[0]
    gl = g_up.shape[0]
    a_cols = 3 * aw + dl + al + gl
    gl_pad = 384
    assert dl == 64 and al == 64 and gl <= gl_pad
    d = w_in.shape[0]
    pad_cols = jnp.zeros((d, gl_pad - gl), F32)
    w_a = jnp.concatenate([w_in[:, :a_cols], pad_cols], axis=1).astype(BF16)
    w_b = w_in[:, a_cols:].astype(BF16)
    mu_p = jnp.concatenate([mu, jnp.zeros((gl_pad - gl,), F32)])[None, :]
    zeros = jnp.zeros((dl, aw), F32)
    wwa = jnp.concatenate([jnp.concatenate([w_up, zeros], 1), jnp.concatenate([zeros, a_up], 1)], 0)
    gup = jnp.concatenate([g_up, jnp.zeros((gl_pad - gl, aw), F32)], 0)

    h_a = _matmul(xbf, w_a, F32, 1024, 512, "proj_rwkv")
    h_b = _matmul(xbf, w_b, F32, 1024, w_b.shape[1], "proj_swa")

    ones = jnp.kron(jnp.eye(2, dtype=F32), jnp.ones((HEAD_DIM, HEAD_DIM), F32)).astype(BF16)
    at, bt, kt, rt, bh, kh, v, g, bvg, wc = _rwkv_prep(
        h_a, mu_p, w0[None], a0[None], k_k[None], k_a[None], r_k.reshape(1, aw), wwa, gup, ones,
        seq, 256)
    y = _rwkv_scan(at, bt, kt, rt, bh, kh, v, wc, batch, seq)
    ya = _rwkv_post(y, g, bvg, gn_w[None], gn_b[None], ones, 512)

    qw = sinks.shape[0] * HEAD_DIM
    kvw = (w_b.shape[1] - qw) // 2
    tabs, half = _rope_tables(positions, HEAD_DIM, 128)
    q, kb, vb = _swa_prep(h_b, tabs, half, qw, kvw, 512)
    yb = _swa(q, kb, vb, sinks, batch, seq)
    return ya, yb


def _odd_mixer(xbf, positions, w_in, lam_p, subln_g, layer_idx, batch, seq):
    width = w_in.shape[1] // 3
    q_scale = DIFF_HEAD_DIM ** -0.5 * math.log2(math.e)
    w = jnp.concatenate([w_in[:, :width] * q_scale, w_in[:, width:]], axis=1).astype(BF16)
    h = _matmul(xbf, w, BF16, 1024, 1024, "proj_diff")
    tabs, half = _rope_tables(positions, DIFF_HEAD_DIM, 128)
    qk = _diff_rope(h, tabs, half, width, 512, 1024)
    lam_init = 0.8 - 0.6 * math.exp(-0.3 * layer_idx)
    lv = lam_p.astype(F32)
    lam = jnp.exp(jnp.sum(lv[0] * lv[1])) - jnp.exp(jnp.sum(lv[2] * lv[3])) + lam_init
    scalars = jnp.stack([lam, jnp.asarray(1.0 - lam_init, F32)]).astype(F32)
    return _diff_attn(qk, h, scalars, subln_g[None], batch, seq, 1024, 1024)


def kernel(x, p, positions, e_w_in, e_mu, e_w0, e_w_up, e_a0, e_a_up, e_g_up, e_k_k, e_k_a, e_r_k,
           e_gn_w, e_gn_b, e_sinks, e_w_out, o_w_in, o_lambda, o_subln_g, o_w_out,
           ln1_g, ln1_b, ln2_g, ln2_b, ffn_w1, ffn_w3, ffn_w2, ple_w_proj, ple_w_gate):
    batch, seq, d = x.shape
    depth = p.shape[0]
    t = batch * seq
    alpha = (2.0 * depth) ** 0.25
    xf = x.reshape(t, d)
    xbf = xf.astype(BF16)
    for i in range(depth):
        j = i // 2
        if i % 2 == 0:
            ya, yb = _even_mixer(xbf, positions, e_w_in[j], e_mu[j], e_w0[j], e_w_up[j], e_a0[j],
                                 e_a_up[j], e_g_up[j], e_k_k[j], e_k_a[j], e_r_k[j], e_gn_w[j],
                                 e_gn_b[j], e_sinks[j], batch, seq)
            aw = ya.shape[1]
            w_out = e_w_out[j].astype(BF16)
            ys, ws = [ya, yb], [w_out[:aw], w_out[aw:]]
        else:
            y = _odd_mixer(xbf, positions, o_w_in[j], o_lambda[j], o_subln_g[j], i, batch, seq)
            ys, ws = [y], [o_w_out[j].astype(BF16)]
        xf, xbf = _outproj_ln(ys, ws, xf, ln1_g[i][None], ln1_b[i][None], alpha, 256)
        act = _ffn_up(xbf, ffn_w1[i].astype(BF16), ffn_w3[i].astype(BF16), 1024, 512)
        xf, xbf = _ffn_down_ple(act, ffn_w2[i].astype(BF16), xf, ln2_g[i][None], ln2_b[i][None],
                                p[i].reshape(t, -1), ple_w_gate[i].astype(BF16),
                                ple_w_proj[i].astype(BF16), alpha, 256)
    return xf.reshape(batch, seq, d)
```

```python
import functools
import math

import jax
import jax.numpy as jnp
from jax import lax
from jax.experimental import pallas as pl
from jax.experimental.pallas import tpu as pltpu

F32 = jnp.float32
BF16 = jnp.bfloat16
HI = lax.Precision.HIGHEST

HEAD_DIM = 64
DIFF_HEAD_DIM = 128
WINDOW = 128
ROPE_THETA = 500000.0
ROPE_FRACTION = 4
LN_EPS = 1e-5
WKV_GN_EPS = 64e-5
SUBLN_EPS = 1e-5
RWKV_CHUNK = 64
RWKV_CHUNKS_PER_STEP = 4
VMEM_LIMIT = 56 * 1024 * 1024


def _dot(a, b, prec=None):
    return jnp.dot(a, b, preferred_element_type=F32, precision=prec)


def _dot_nt(a, b, prec=None):
    return lax.dot_general(a, b, (((1,), (1,)), ((), ())), preferred_element_type=F32, precision=prec)


def _dot_tn(a, b, prec=None):
    return lax.dot_general(a, b, (((0,), (0,)), ((), ())), preferred_element_type=F32, precision=prec)


def _params(*sem):
    return pltpu.CompilerParams(dimension_semantics=sem, vmem_limit_bytes=VMEM_LIMIT)


def _layer_norm(z, g, b):
    mu = jnp.mean(z, -1, keepdims=True)
    d = z - mu
    var = jnp.mean(d * d, -1, keepdims=True)
    return d * lax.rsqrt(var + LN_EPS) * g + b


def _mm_kernel(x_ref, w_ref, o_ref):
    o_ref[...] = _dot(x_ref[...], w_ref[...]).astype(o_ref.dtype)


def _matmul(x, w, out_dtype, tm, tn, name):
    t, k = x.shape
    n = w.shape[1]
    return pl.pallas_call(
        _mm_kernel,
        grid=(t // tm, n // tn),
        in_specs=[pl.BlockSpec((tm, k), lambda i, j: (i, 0)),
                  pl.BlockSpec((k, tn), lambda i, j: (0, j))],
        out_specs=pl.BlockSpec((tm, tn), lambda i, j: (i, j)),
        out_shape=jax.ShapeDtypeStruct((t, n), out_dtype),
        compiler_params=_params("parallel", "arbitrary"),
        name=name,
    )(x, w)


def _bf16_parts(x, n):
    parts = []
    for _ in range(n):
        p = x.astype(BF16)
        parts.append(p)
        x = x - p.astype(F32)
    return parts


def _split_dot(x, w, n=2):
    return sum(_dot(p, w) for p in _bf16_parts(x, n))


def _dot3(a, b):
    ah, al = _bf16_parts(a, 2)
    bh, bl = _bf16_parts(b, 2)
    return _dot(ah, bh) + _dot(ah, bl) + _dot(al, bh)


def _head_sums(x, ones):
    gw = ones.shape[0]
    parts = [_split_dot(x[:, i:i + gw], ones) for i in range(0, x.shape[1], gw)]
    return jnp.concatenate(parts, axis=1)


def _rwkv_prep_kernel(h_ref, hp_ref, mu_ref, w0_ref, a0_ref, kk_ref, ka_ref, rk_ref, wwa_ref,
                      gup_ref, ones_ref, at_out, bt_out, kt_out, rt_out, bh_out, kh_out, v_out,
                      g_out, bvg_out, wc_out, *, blocks_per_seq, aw, chunk):
    i = pl.program_id(0)
    h = h_ref[...]
    tm = h.shape[0]
    first = (i % blocks_per_seq) == 0
    prev_row = jnp.where(first, 0.0, hp_ref[7:8, :])
    shifted = pltpu.roll(h, 1, axis=0)
    row = lax.broadcasted_iota(jnp.int32, h.shape, 0)
    shifted = jnp.where(row == 0, prev_row, shifted)
    hm = h + (shifted - h) * mu_ref[...]
    r = hm[:, 0:aw]
    k = hm[:, aw:2 * aw]
    v = hm[:, 2 * aw:3 * aw]
    z = hm[:, 3 * aw:3 * aw + 128]
    lane = lax.broadcasted_iota(jnp.int32, z.shape, 1)
    z = jnp.where(lane < 64, jnp.tanh(z), z)
    wa = _dot3(z, wwa_ref[...])
    w = -jax.nn.softplus(-(w0_ref[...] + wa[:, :aw])) - 0.5
    a = jax.nn.sigmoid(a0_ref[...] + wa[:, aw:])
    lg = hm[:, 3 * aw + 128:]
    g = _dot3(jax.nn.sigmoid(lg), gup_ref[...])
    lw = -jnp.exp(w)

    ri = lax.broadcasted_iota(jnp.int32, (tm, tm), 0)
    ci = lax.broadcasted_iota(jnp.int32, (tm, tm), 1)
    shift = int(math.log2(chunk))
    same = (ri >> shift) == (ci >> shift)
    lw_parts = _bf16_parts(lw, 3)
    tri = jnp.where(same & (ri >= ci), 1.0, 0.0).astype(BF16)
    blk = jnp.where(same, 1.0, 0.0).astype(BF16)
    cs = sum(_dot(tri, part) for part in lw_parts)
    tot = sum(_dot(blk, part) for part in lw_parts)

    ones = ones_ref[...]
    kkr = k * kk_ref[...]
    kk = kkr / jnp.maximum(jnp.sqrt(_head_sums(kkr * kkr, ones)), 1e-12)
    k2 = k * (1.0 + (a - 1.0) * ka_ref[...])
    bb = kk * a
    w_inv = jnp.exp(-cs)
    w_rem = jnp.exp(tot - cs)
    at_out[...] = (-kk * jnp.exp(cs - lw)).astype(BF16)
    bt_out[...] = (bb * w_inv).astype(BF16)
    kt_out[...] = (k2 * w_inv).astype(BF16)
    rt_out[...] = (r * jnp.exp(cs)).astype(BF16)
    bh_out[...] = (bb * w_rem).astype(BF16)
    kh_out[...] = (k2 * w_rem).astype(BF16)
    v_out[...] = v.astype(BF16)
    g_out[...] = g
    bvg_out[...] = _head_sums(r * k2 * rk_ref[...], ones) * v * g
    for c in range(tm // chunk):
        wc_out[c] = jnp.exp(tot[c * chunk:c * chunk + 1, :])


def _rwkv_prep(h, mu, w0, a0, k_k, k_a, r_k, wwa, gup, ones, seq, tm):
    t, cols = h.shape
    aw = w0.shape[1]
    chunk = RWKV_CHUNK
    row_spec = pl.BlockSpec((tm, aw), lambda i: (i, 0))
    full = lambda arr: pl.BlockSpec(arr.shape, lambda i: (0, 0))
    kern = functools.partial(_rwkv_prep_kernel, blocks_per_seq=seq // tm, aw=aw, chunk=chunk)
    return pl.pallas_call(
        kern,
        grid=(t // tm,),
        in_specs=[pl.BlockSpec((tm, cols), lambda i: (i, 0)),
                  pl.BlockSpec((8, cols), lambda i: (jnp.maximum(i * (tm // 8) - 1, 0), 0)),
                  full(mu), full(w0), full(a0), full(k_k), full(k_a), full(r_k), full(wwa),
                  full(gup), full(ones)],
        out_specs=[row_spec] * 9 + [pl.BlockSpec((tm // chunk, 1, aw), lambda i: (i, 0, 0))],
        out_shape=([jax.ShapeDtypeStruct((t, aw), BF16)] * 7 + [jax.ShapeDtypeStruct((t, aw), F32)] * 2
                   + [jax.ShapeDtypeStruct((t // chunk, 1, aw), F32)]),
        compiler_params=_params("parallel"),
        name="rwkv_prep",
    )(h, h, mu, w0, a0, k_k, k_a, r_k, wwa, gup, ones)


def _rwkv_scan_kernel(at_ref, bt_ref, kt_ref, rt_ref, bh_ref, kh_ref, v_ref, wc_ref, y_ref, st_ref,
                      *, pairs, n):
    c = n
    n2 = 2 * n
    nsub = at_ref.shape[0] // c

    @pl.when(pl.program_id(1) == 0)
    def _():
        st_ref[...] = jnp.zeros_like(st_ref)

    row = lax.broadcasted_iota(jnp.int32, (n2, n2), 0)
    col = lax.broadcasted_iota(jnp.int32, (n2, n2), 1)
    same = (row >= c) == (col >= c)
    strict = same & (row > col)
    incl = same & (row >= col)
    incl2 = jnp.concatenate([incl, incl], axis=1)
    eye = (row == col).astype(F32)
    lo = lax.broadcasted_iota(jnp.int32, (c, n2), 1) < n

    def stack(ref, q, sl):
        x = ref[q * c:(q + 1) * c, sl]
        zero = jnp.zeros_like(x)
        return jnp.concatenate([jnp.where(lo, x, zero), jnp.where(lo, zero, x)], axis=0)

    sls = [slice(p * n2, (p + 1) * n2) for p in range(pairs)]
    units = [(q, p) for q in range(nsub) for p in range(pairs)]
    stacks = {u: [stack(ref, u[0], sls[u[1]])
                  for ref in (at_ref, bt_ref, kt_ref, rt_ref, bh_ref, kh_ref, v_ref)] for u in units}
    ar = {u: jnp.concatenate([stacks[u][0], stacks[u][3]], axis=0) for u in units}
    big = {u: _dot_nt(ar[u], jnp.concatenate([stacks[u][1], stacks[u][2]], axis=0)) for u in units}
    l_ab = {u: jnp.where(strict, big[u][:n2, :n2], 0.0) for u in units}
    l_ak = {u: jnp.where(strict, big[u][:n2, n2:], 0.0).astype(BF16) for u in units}
    m_all = {u: jnp.where(incl2, big[u][n2:, :], 0.0).astype(BF16) for u in units}

    tinv = {u: eye + l_ab[u] for u in units}
    pw = {u: l_ab[u].astype(BF16) for u in units}
    for _ in range(int(math.log2(c)) - 1):
        pw = {u: _dot(pw[u], pw[u]).astype(BF16) for u in units}
        tinv = {u: tinv[u] + _dot(tinv[u].astype(BF16), pw[u]) for u in units}
    tinv = {u: tinv[u].astype(BF16) for u in units}
    lv = {u: _dot(l_ak[u], stacks[u][6]) for u in units}

    state = [st_ref[p] for p in range(pairs)]
    for q in range(nsub):
        us = [(q, p) for p in range(pairs)]
        from_state = [_dot_nt(ar[u], state[u[1]].astype(BF16)) for u in us]
        x1 = [(from_state[p][:n2] + lv[us[p]]).astype(BF16) for p in range(pairs)]
        uu = [_dot(tinv[us[p]], x1[p]).astype(BF16) for p in range(pairs)]
        uv = [jnp.concatenate([uu[p], stacks[us[p]][6]], axis=0) for p in range(pairs)]
        for p in range(pairs):
            u = us[p]
            yst = from_state[p][n2:] + _dot(m_all[u], uv[p])
            state[p] = state[p] * wc_ref[q, :, sls[p]] + _dot_tn(
                uv[p], jnp.concatenate([stacks[u][4], stacks[u][5]], axis=0))
            y_ref[q * c:(q + 1) * c, sls[p]] = yst[:c] + yst[c:]
    for p in range(pairs):
        st_ref[p] = state[p]


def _rwkv_scan(at, bt, kt, rt, bh, kh, v, wc, batch, seq):
    t, aw = at.shape
    c = RWKV_CHUNK
    n = HEAD_DIM
    assert c == n
    pairs = aw // (2 * n)
    nsub = RWKV_CHUNKS_PER_STEP
    nstep = seq // (c * nsub)
    tok = pl.BlockSpec((c * nsub, aw), lambda b, j: (b * nstep + j, 0))
    kern = functools.partial(_rwkv_scan_kernel, pairs=pairs, n=n)
    return pl.pallas_call(
        kern,
        grid=(batch, nstep),
        in_specs=[tok] * 7 + [pl.BlockSpec((nsub, 1, aw), lambda b, j: (b * nstep + j, 0, 0))],
        out_specs=tok,
        out_shape=jax.ShapeDtypeStruct((t, aw), F32),
        scratch_shapes=[pltpu.VMEM((pairs, 2 * n, 2 * n), F32)],
        compiler_params=_params("parallel", "arbitrary"),
        name="rwkv_scan",
    )(at, bt, kt, rt, bh, kh, v, wc)


def _rwkv_post_kernel(y_ref, g_ref, bvg_ref, gnw_ref, gnb_ref, ones_ref, o_ref, *, n):
    y = y_ref[...]
    ones = ones_ref[...]
    ym = _head_sums(y, ones) * (1.0 / n)
    d = y - ym
    yv = _head_sums(d * d, ones) * (1.0 / n)
    yn = d * lax.rsqrt(yv + WKV_GN_EPS) * gnw_ref[...] + gnb_ref[...]
    o_ref[...] = (yn * g_ref[...] + bvg_ref[...]).astype(o_ref.dtype)


def _rwkv_post(y, g, bvg, gn_w, gn_b, ones, tm):
    t, aw = y.shape
    row = pl.BlockSpec((tm, aw), lambda i: (i, 0))
    full = lambda arr: pl.BlockSpec(arr.shape, lambda i: (0, 0))
    return pl.pallas_call(
        functools.partial(_rwkv_post_kernel, n=HEAD_DIM),
        grid=(t // tm,),
        in_specs=[row, row, row, full(gn_w), full(gn_b), full(ones)],
        out_specs=row,
        out_shape=jax.ShapeDtypeStruct((t, aw), BF16),
        compiler_params=_params("parallel"),
        name="rwkv_post",
    )(y, g, bvg, gn_w, gn_b, ones)


def _rope_tables(positions, head_dim, lanes):
    rd = head_dim // ROPE_FRACTION
    half = rd // 2
    inv_freq = ROPE_THETA ** (-jnp.arange(half, dtype=F32) / half)
    ang = positions.astype(F32).reshape(-1, 1) * inv_freq
    cos, sin = jnp.cos(ang), jnp.sin(ang)
    t = ang.shape[0]
    pad = jnp.zeros((t, head_dim - rd), F32)
    zero = jnp.zeros((t, half), F32)
    c = jnp.concatenate([cos, cos, pad + 1.0], -1)
    sn = jnp.concatenate([-sin, zero, pad], -1)
    sp = jnp.concatenate([zero, sin, pad], -1)
    rep = lanes // head_dim
    return tuple(jnp.tile(x, (1, rep)) for x in (c, sn, sp)), half


def _rope(x, c, sn, sp, half):
    width = x.shape[1]
    rep = width // c.shape[1]
    if rep > 1:
        c, sn, sp = (jnp.concatenate([y] * rep, axis=1) for y in (c, sn, sp))
    nxt = pltpu.roll(x, width - half, axis=1)
    prv = pltpu.roll(x, half, axis=1)
    return x * c + nxt * sn + prv * sp


def _swa_prep_kernel(q_ref, kv_ref, c_ref, sn_ref, sp_ref, q_out, k_out, v_out, *, half, scale):
    c, sn, sp = c_ref[...], sn_ref[...], sp_ref[...]
    q_out[...] = (_rope(q_ref[...], c, sn, sp, half) * scale).astype(q_out.dtype)
    kv = kv_ref[...]
    kw = kv.shape[1] // 2
    k_out[...] = _rope(kv[:, :kw], c, sn, sp, half).astype(k_out.dtype)
    v_out[...] = kv[:, kw:].astype(v_out.dtype)


def _swa_prep(hb, tabs, half, qw, kvw, tm):
    t = hb.shape[0]
    kern = functools.partial(_swa_prep_kernel, half=half, scale=HEAD_DIM ** -0.5)
    tab_spec = pl.BlockSpec((tm, 128), lambda i: (i, 0))
    return pl.pallas_call(
        kern,
        grid=(t // tm,),
        in_specs=[pl.BlockSpec((tm, qw), lambda i: (i, 0)),
                  pl.BlockSpec((tm, 2 * kvw), lambda i: (i, qw // (2 * kvw))),
                  tab_spec, tab_spec, tab_spec],
        out_specs=[pl.BlockSpec((tm, qw), lambda i: (i, 0)),
                   pl.BlockSpec((tm, kvw), lambda i: (i, 0)),
                   pl.BlockSpec((tm, kvw), lambda i: (i, 0))],
        out_shape=[jax.ShapeDtypeStruct((t, qw), BF16),
                   jax.ShapeDtypeStruct((t, kvw), BF16),
                   jax.ShapeDtypeStruct((t, kvw), BF16)],
        compiler_params=_params("parallel"),
        name="swa_prep",
    )(hb, hb, *tabs)


def _swa_kernel(sink_ref, q_ref, kp_ref, kc_ref, vp_ref, vc_ref, o_ref, *, q_heads, kv_heads, n):
    blk = q_ref.shape[0]
    j = pl.program_id(1)
    kw = jnp.concatenate([kp_ref[...], kc_ref[...]], axis=0)
    vw = jnp.concatenate([vp_ref[...], vc_ref[...]], axis=0)
    qi = lax.broadcasted_iota(jnp.int32, (blk, 2 * blk), 0)
    kj = lax.broadcasted_iota(jnp.int32, (blk, 2 * blk), 1)
    rel = qi + blk - kj
    mask = (rel >= 0) & (rel < WINDOW) & ((kj >= blk) | (j > 0))
    group = q_heads // kv_heads
    ss = [_dot_nt(q_ref[:, h * n:(h + 1) * n], kw[:, (h // group) * n:(h // group + 1) * n])
          for h in range(q_heads)]
    ps = []
    for h in range(q_heads):
        s = jnp.where(mask, ss[h], -jnp.inf)
        sink = sink_ref[h]
        m = jnp.maximum(jnp.max(s, -1, keepdims=True), sink)
        e = jnp.exp(s - m)
        den = jnp.sum(e, -1, keepdims=True) + jnp.exp(sink - m)
        ps.append((e / den).astype(vw.dtype))
    for h in range(q_heads):
        g = h // group
        o_ref[:, h * n:(h + 1) * n] = _dot(ps[h], vw[:, g * n:(g + 1) * n]).astype(o_ref.dtype)


def _swa(q, k, v, sinks, batch, seq):
    t, qw = q.shape
    kvw = k.shape[1]
    n = HEAD_DIM
    blk = WINDOW
    nb = seq // blk
    cur = lambda b, j: (b * nb + j, 0)
    prev = lambda b, j: (b * nb + jnp.maximum(j - 1, 0), 0)
    kern = functools.partial(_swa_kernel, q_heads=qw // n, kv_heads=kvw // n, n=n)
    return pl.pallas_call(
        kern,
        grid=(batch, nb),
        in_specs=[pl.BlockSpec(memory_space=pltpu.SMEM),
                  pl.BlockSpec((blk, qw), cur),
                  pl.BlockSpec((blk, kvw), prev), pl.BlockSpec((blk, kvw), cur),
                  pl.BlockSpec((blk, kvw), prev), pl.BlockSpec((blk, kvw), cur)],
        out_specs=pl.BlockSpec((blk, qw), cur),
        out_shape=jax.ShapeDtypeStruct((t, qw), F32),
        compiler_params=_params("parallel", "arbitrary"),
        name="swa",
    )(sinks, q, k, k, v, v)


def _diff_rope_kernel(x_ref, c_ref, sn_ref, sp_ref, o_ref, *, half):
    x = x_ref[...].astype(F32)
    o_ref[...] = _rope(x, c_ref[...], sn_ref[...], sp_ref[...], half).astype(o_ref.dtype)


def _diff_rope(h, tabs, half, width, tm, tn):
    t = h.shape[0]
    tab_spec = pl.BlockSpec((tm, 128), lambda i, j: (i, 0))
    blk = pl.BlockSpec((tm, tn), lambda i, j: (i, j))
    return pl.pallas_call(
        functools.partial(_diff_rope_kernel, half=half),
        grid=(t // tm, 2 * width // tn),
        in_specs=[blk, tab_spec, tab_spec, tab_spec],
        out_specs=blk,
        out_shape=jax.ShapeDtypeStruct((t, 2 * width), h.dtype),
        compiler_params=_params("parallel", "arbitrary"),
        name="diff_rope",
    )(h, *tabs)


def _diff_attn_kernel(sc_ref, q_ref, k_ref, v_ref, g_ref, o_ref, m_ref, l_ref, acc_ref, *, bq, bk, rg,
                      dh):
    qi = pl.program_id(2)
    m_ref[...] = jnp.full_like(m_ref, -jnp.inf)
    l_ref[...] = jnp.zeros_like(l_ref)
    acc_ref[...] = jnp.zeros_like(acc_ref)
    ratio = bq // bk

    def block(j, diag_offset):
        off = pl.multiple_of(j * bk, bk)
        v = v_ref[pl.ds(off, bk), :]
        rows = [slice(r * rg, (r + 1) * rg) for r in range(bq // rg)]
        ss = [[_dot_nt(q_ref[rs, c * dh:(c + 1) * dh], k_ref[pl.ds(off, bk), c * dh:(c + 1) * dh])
               for c in range(2)] for rs in rows]
        ps, alphas = [], []
        for r, rs in enumerate(rows):
            if diag_offset is not None:
                row = lax.broadcasted_iota(jnp.int32, (rg, bk), 0) + r * rg
                col = lax.broadcasted_iota(jnp.int32, (rg, bk), 1) + diag_offset
                ss[r] = [jnp.where(col <= row, s, -jnp.inf) for s in ss[r]]
            pr, ar = [], []
            for c in range(2):
                m_old = m_ref[c, rs, :]
                m_new = jnp.maximum(m_old, jnp.max(ss[r][c], -1, keepdims=True))
                alpha = jnp.exp2(m_old - m_new)
                p = jnp.exp2(ss[r][c] - jnp.concatenate([m_new] * (bk // 128), axis=1))
                l_ref[c, rs, :] = alpha * l_ref[c, rs, :] + jnp.sum(p, -1, keepdims=True)
                m_ref[c, rs, :] = m_new
                pr.append(p.astype(v.dtype))
                ar.append(jnp.concatenate([alpha] * (2 * dh // 128), axis=1))
            ps.append(pr)
            alphas.append(ar)
        for r, rs in enumerate(rows):
            for c in range(2):
                acc_ref[c, rs, :] = alphas[r][c] * acc_ref[c, rs, :] + _dot(ps[r][c], v)

    def body(j, carry):
        block(j, None)
        return carry

    lax.fori_loop(0, qi * ratio, body, 0)
    for h in range(ratio):
        block(qi * ratio + h, h * bk)

    lam = sc_ref[0]
    post = sc_ref[1]
    rep = 2 * dh // 128
    o1 = acc_ref[0] / jnp.concatenate([l_ref[0]] * rep, axis=1)
    o2 = acc_ref[1] / jnp.concatenate([l_ref[1]] * rep, axis=1)
    out = o1 - lam * o2
    out = out * lax.rsqrt(jnp.mean(out * out, -1, keepdims=True) + SUBLN_EPS)
    o_ref[...] = (out * g_ref[...] * post).astype(o_ref.dtype)


def _diff_attn(qk, hv, scalars, subln_g, batch, seq, bq, bk, rg):
    t = qk.shape[0]
    width = qk.shape[1] // 2
    dh = DIFF_HEAD_DIM
    hw = 2 * dh
    heads = width // hw
    nq = seq // bq
    kern = functools.partial(_diff_attn_kernel, bq=bq, bk=bk, rg=rg, dh=dh)
    resident = lambda first: pl.BlockSpec((seq, hw), lambda b, h, qi: (b, first + h),
                                          pipeline_mode=pl.Buffered(1))
    return pl.pallas_call(
        kern,
        grid=(batch, heads, nq),
        in_specs=[pl.BlockSpec(memory_space=pltpu.SMEM),
                  pl.BlockSpec((bq, hw), lambda b, h, qi: (b * nq + qi, h)),
                  resident(heads), resident(2 * heads),
                  pl.BlockSpec((1, hw), lambda b, h, qi: (0, 0))],
        out_specs=pl.BlockSpec((bq, hw), lambda b, h, qi: (b * nq + qi, h)),
        out_shape=jax.ShapeDtypeStruct((t, width), BF16),
        scratch_shapes=[pltpu.VMEM((2, bq, 128), F32), pltpu.VMEM((2, bq, 128), F32),
                        pltpu.VMEM((2, bq, hw), F32)],
        compiler_params=_params("parallel", "parallel", "arbitrary"),
        name="diff_attn",
    )(scalars, qk, qk, hv, subln_g)


def _outproj_ln_kernel(*refs, n_in, alpha):
    ys = refs[:n_in]
    ws = refs[n_in:2 * n_in]
    x_ref, g_ref, b_ref, o_ref, obf_ref = refs[2 * n_in:]
    mix = _dot(ys[0][...].astype(BF16), ws[0][...])
    for y, w in zip(ys[1:], ws[1:]):
        mix = mix + _dot(y[...].astype(BF16), w[...])
    out = _layer_norm(alpha * x_ref[...] + mix, g_ref[...], b_ref[...])
    o_ref[...] = out
    obf_ref[...] = out.astype(BF16)


def _outproj_ln(ys, ws, x, g, b, alpha, tm):
    t, d = x.shape
    n_in = len(ys)
    kern = functools.partial(_outproj_ln_kernel, n_in=n_in, alpha=alpha)
    row = lambda width: pl.BlockSpec((tm, width), lambda i: (i, 0))
    full = lambda arr: pl.BlockSpec(arr.shape, lambda i: (0, 0), pipeline_mode=pl.Buffered(1))
    return pl.pallas_call(
        kern,
        grid=(t // tm,),
        in_specs=[row(y.shape[1]) for y in ys] + [full(w) for w in ws] + [row(d), full(g), full(b)],
        out_specs=[row(d), row(d)],
        out_shape=[jax.ShapeDtypeStruct((t, d), F32), jax.ShapeDtypeStruct((t, d), BF16)],
        compiler_params=_params("parallel"),
        name="outproj_ln",
    )(*ys, *ws, x, g, b)


def _ffn_up_kernel(x_ref, w1_ref, w3_ref, o_ref):
    x = x_ref[...]
    h1 = _dot(x, w1_ref[...])
    h3 = _dot(x, w3_ref[...])
    o_ref[...] = (h1 * jax.nn.sigmoid(h1) * h3).astype(o_ref.dtype)


def _ffn_up(x, w1, w3, tm, tf):
    t, d = x.shape
    f = w1.shape[1]
    return pl.pallas_call(
        _ffn_up_kernel,
        grid=(t // tm, f // tf),
        in_specs=[pl.BlockSpec((tm, d), lambda i, j: (i, 0)),
                  pl.BlockSpec((d, tf), lambda i, j: (0, j)),
                  pl.BlockSpec((d, tf), lambda i, j: (0, j))],
        out_specs=pl.BlockSpec((tm, tf), lambda i, j: (i, j)),
        out_shape=jax.ShapeDtypeStruct((t, f), BF16),
        compiler_params=_params("parallel", "arbitrary"),
        name="ffn_up",
    )(x, w1, w3)


def _ffn_down_ple_kernel(a_ref, w2_ref, x_ref, g_ref, b_ref, p_ref, wg_ref, wp_ref, o_ref, obf_ref,
                         *, alpha):
    y = _dot(a_ref[...], w2_ref[...])
    x2 = _layer_norm(alpha * x_ref[...] + y, g_ref[...], b_ref[...])
    gate = jax.nn.sigmoid(_dot(x2.astype(BF16), wg_ref[...]))
    proj = _dot(p_ref[...].astype(BF16), wp_ref[...])
    out = x2 + gate * proj
    o_ref[...] = out
    obf_ref[...] = out.astype(BF16)


def _ffn_down_ple(act, w2, x, g, b, p, wg, wp, alpha, tm):
    t, d = x.shape
    f = act.shape[1]
    pd = p.shape[1]
    kern = functools.partial(_ffn_down_ple_kernel, alpha=alpha)
    row = lambda width: pl.BlockSpec((tm, width), lambda i: (i, 0))
    once = lambda arr: pl.BlockSpec(arr.shape, lambda i: (0, 0), pipeline_mode=pl.Buffered(1))
    return pl.pallas_call(
        kern,
        grid=(t // tm,),
        in_specs=[row(f), once(w2), row(d), once(g), once(b), row(pd), once(wg), once(wp)],
        out_specs=[row(d), row(d)],
        out_shape=[jax.ShapeDtypeStruct((t, d), F32), jax.ShapeDtypeStruct((t, d), BF16)],
        compiler_params=_params("parallel"),
        name="ffn_down_ple",
    )(act, w2, x, g, b, p, wg, wp)


def _even_mixer(xbf, positions, w_in, mu, w0, w_up, a0, a_up, g_up, k_k, k_a, r_k, gn_w, gn_b,
                sinks, batch, seq):
    aw = w0.shape[0]
    dl = w_up.shape[0]
    al = a_up.shape[0]
    gl = g_up.shape[0]
    a_cols = 3 * aw + dl + al + gl
    gl_pad = 384
    assert dl == 64 and al == 64 and gl <= gl_pad
    d = w_in.shape[0]
    pad_cols = jnp.zeros((d, gl_pad - gl), F32)
    w_a = jnp.concatenate([w_in[:, :a_cols], pad_cols], axis=1).astype(BF16)
    w_b = w_in[:, a_cols:].astype(BF16)
    mu_p = jnp.concatenate([mu, jnp.zeros((gl_pad - gl,), F32)])[None, :]
    zeros = jnp.zeros((dl, aw), F32)
    wwa = jnp.concatenate([jnp.concatenate([w_up, zeros], 1), jnp.concatenate([zeros, a_up], 1)], 0)
    gup = jnp.concatenate([g_up, jnp.zeros((gl_pad - gl, aw), F32)], 0)

    h_a = _matmul(xbf, w_a, F32, 1024, 512, "proj_rwkv")
    h_b = _matmul(xbf, w_b, F32, 1024, w_b.shape[1], "proj_swa")

    ones = jnp.kron(jnp.eye(2, dtype=F32), jnp.ones((HEAD_DIM, HEAD_DIM), F32)).astype(BF16)
    at, bt, kt, rt, bh, kh, v, g, bvg, wc = _rwkv_prep(
        h_a, mu_p, w0[None], a0[None], k_k[None], k_a[None], r_k.reshape(1, aw), wwa, gup, ones,
        seq, 256)
    y = _rwkv_scan(at, bt, kt, rt, bh, kh, v, wc, batch, seq)
    ya = _rwkv_post(y, g, bvg, gn_w[None], gn_b[None], ones, 512)

    qw = sinks.shape[0] * HEAD_DIM
    kvw = (w_b.shape[1] - qw) // 2
    tabs, half = _rope_tables(positions, HEAD_DIM, 128)
    q, kb, vb = _swa_prep(h_b, tabs, half, qw, kvw, 512)
    yb = _swa(q, kb, vb, sinks, batch, seq)
    return ya, yb


def _odd_mixer(xbf, positions, w_in, lam_p, subln_g, layer_idx, batch, seq):
    width = w_in.shape[1] // 3
    q_scale = DIFF_HEAD_DIM ** -0.5 * math.log2(math.e)
    w = jnp.concatenate([w_in[:, :width] * q_scale, w_in[:, width:]], axis=1).astype(BF16)
    h = _matmul(xbf, w, BF16, 1024, 1024, "proj_diff")
    tabs, half = _rope_tables(positions, DIFF_HEAD_DIM, 128)
    qk = _diff_rope(h, tabs, half, width, 512, 1024)
    lam_init = 0.8 - 0.6 * math.exp(-0.3 * layer_idx)
    lv = lam_p.astype(F32)
    lam = jnp.exp(jnp.sum(lv[0] * lv[1])) - jnp.exp(jnp.sum(lv[2] * lv[3])) + lam_init
    scalars = jnp.stack([lam, jnp.asarray(1.0 - lam_init, F32)]).astype(F32)
    return _diff_attn(qk, h, scalars, subln_g[None], batch, seq, 1024, 1024, 256)


def kernel(x, p, positions, e_w_in, e_mu, e_w0, e_w_up, e_a0, e_a_up, e_g_up, e_k_k, e_k_a, e_r_k,
           e_gn_w, e_gn_b, e_sinks, e_w_out, o_w_in, o_lambda, o_subln_g, o_w_out,
           ln1_g, ln1_b, ln2_g, ln2_b, ffn_w1, ffn_w3, ffn_w2, ple_w_proj, ple_w_gate):
    batch, seq, d = x.shape
    depth = p.shape[0]
    t = batch * seq
    alpha = (2.0 * depth) ** 0.25
    xf = x.reshape(t, d)
    xbf = xf.astype(BF16)
    for i in range(depth):
        j = i // 2
        if i % 2 == 0:
            ya, yb = _even_mixer(xbf, positions, e_w_in[j], e_mu[j], e_w0[j], e_w_up[j], e_a0[j],
                                 e_a_up[j], e_g_up[j], e_k_k[j], e_k_a[j], e_r_k[j], e_gn_w[j],
                                 e_gn_b[j], e_sinks[j], batch, seq)
            aw = ya.shape[1]
            w_out = e_w_out[j].astype(BF16)
            ys, ws = [ya, yb], [w_out[:aw], w_out[aw:]]
        else:
            y = _odd_mixer(xbf, positions, o_w_in[j], o_lambda[j], o_subln_g[j], i, batch, seq)
            ys, ws = [y], [o_w_out[j].astype(BF16)]
        xf, xbf = _outproj_ln(ys, ws, xf, ln1_g[i][None], ln1_b[i][None], alpha, 512)
        act = _ffn_up(xbf, ffn_w1[i].astype(BF16), ffn_w3[i].astype(BF16), 1024, 512)
        xf, xbf = _ffn_down_ple(act, ffn_w2[i].astype(BF16), xf, ln2_g[i][None], ln2_b[i][None],
                                p[i].reshape(t, -1), ple_w_gate[i].astype(BF16),
                                ple_w_proj[i].astype(BF16), alpha, 256)
    return xf.reshape(batch, seq, d)
```

```python
import functools
import math

import jax
import jax.numpy as jnp
from jax import lax
from jax.experimental import pallas as pl
from jax.experimental.pallas import tpu as pltpu

F32 = jnp.float32
BF16 = jnp.bfloat16
HI = lax.Precision.HIGHEST

HEAD_DIM = 64
DIFF_HEAD_DIM = 128
WINDOW = 128
ROPE_THETA = 500000.0
ROPE_FRACTION = 4
LN_EPS = 1e-5
WKV_GN_EPS = 64e-5
SUBLN_EPS = 1e-5
RWKV_CHUNK = 64
RWKV_CHUNKS_PER_STEP = 4
VMEM_LIMIT = 56 * 1024 * 1024


def _dot(a, b, prec=None):
    return jnp.dot(a, b, preferred_element_type=F32, precision=prec)


def _dot_nt(a, b, prec=None):
    return lax.dot_general(a, b, (((1,), (1,)), ((), ())), preferred_element_type=F32, precision=prec)


def _dot_tn(a, b, prec=None):
    return lax.dot_general(a, b, (((0,), (0,)), ((), ())), preferred_element_type=F32, precision=prec)


def _params(*sem):
    return pltpu.CompilerParams(dimension_semantics=sem, vmem_limit_bytes=VMEM_LIMIT)


def _layer_norm(z, g, b):
    mu = jnp.mean(z, -1, keepdims=True)
    d = z - mu
    var = jnp.mean(d * d, -1, keepdims=True)
    return d * lax.rsqrt(var + LN_EPS) * g + b


def _mm_kernel(x_ref, w_ref, o_ref):
    o_ref[...] = _dot(x_ref[...], w_ref[...]).astype(o_ref.dtype)


def _matmul(x, w, out_dtype, tm, tn, name):
    t, k = x.shape
    n = w.shape[1]
    return pl.pallas_call(
        _mm_kernel,
        grid=(t // tm, n // tn),
        in_specs=[pl.BlockSpec((tm, k), lambda i, j: (i, 0)),
                  pl.BlockSpec((k, tn), lambda i, j: (0, j))],
        out_specs=pl.BlockSpec((tm, tn), lambda i, j: (i, j)),
        out_shape=jax.ShapeDtypeStruct((t, n), out_dtype),
        compiler_params=_params("parallel", "arbitrary"),
        name=name,
    )(x, w)


def _bf16_parts(x, n):
    parts = []
    for _ in range(n):
        p = x.astype(BF16)
        parts.append(p)
        x = x - p.astype(F32)
    return parts


def _split_dot(x, w, n=2):
    return sum(_dot(p, w) for p in _bf16_parts(x, n))


def _dot3(a, b):
    ah, al = _bf16_parts(a, 2)
    bh, bl = _bf16_parts(b, 2)
    return _dot(ah, bh) + _dot(ah, bl) + _dot(al, bh)


def _head_sums(x, ones):
    gw = ones.shape[0]
    parts = [_split_dot(x[:, i:i + gw], ones) for i in range(0, x.shape[1], gw)]
    return jnp.concatenate(parts, axis=1)


def _rwkv_prep_kernel(h_ref, hp_ref, mu_ref, w0_ref, a0_ref, kk_ref, ka_ref, rk_ref, wwa_ref,
                      gup_ref, ones_ref, at_out, bt_out, kt_out, rt_out, bh_out, kh_out, v_out,
                      g_out, bvg_out, wc_out, *, blocks_per_seq, aw, chunk):
    i = pl.program_id(0)
    h = h_ref[...]
    tm = h.shape[0]
    first = (i % blocks_per_seq) == 0
    prev_row = jnp.where(first, 0.0, hp_ref[7:8, :])
    shifted = pltpu.roll(h, 1, axis=0)
    row = lax.broadcasted_iota(jnp.int32, h.shape, 0)
    shifted = jnp.where(row == 0, prev_row, shifted)
    hm = h + (shifted - h) * mu_ref[...]
    r = hm[:, 0:aw]
    k = hm[:, aw:2 * aw]
    v = hm[:, 2 * aw:3 * aw]
    z = hm[:, 3 * aw:3 * aw + 128]
    lane = lax.broadcasted_iota(jnp.int32, z.shape, 1)
    z = jnp.where(lane < 64, jnp.tanh(z), z)
    wa = _dot3(z, wwa_ref[...])
    w = -jax.nn.softplus(-(w0_ref[...] + wa[:, :aw])) - 0.5
    a = jax.nn.sigmoid(a0_ref[...] + wa[:, aw:])
    lg = hm[:, 3 * aw + 128:]
    g = _dot3(jax.nn.sigmoid(lg), gup_ref[...])
    lw = -jnp.exp(w)

    ri = lax.broadcasted_iota(jnp.int32, (tm, tm), 0)
    ci = lax.broadcasted_iota(jnp.int32, (tm, tm), 1)
    shift = int(math.log2(chunk))
    same = (ri >> shift) == (ci >> shift)
    lw_parts = _bf16_parts(lw, 3)
    tri = jnp.where(same & (ri >= ci), 1.0, 0.0).astype(BF16)
    blk = jnp.where(same, 1.0, 0.0).astype(BF16)
    cs = sum(_dot(tri, part) for part in lw_parts)
    tot = sum(_dot(blk, part) for part in lw_parts)

    ones = ones_ref[...]
    kkr = k * kk_ref[...]
    kk = kkr / jnp.maximum(jnp.sqrt(_head_sums(kkr * kkr, ones)), 1e-12)
    k2 = k * (1.0 + (a - 1.0) * ka_ref[...])
    bb = kk * a
    w_inv = jnp.exp(-cs)
    w_rem = jnp.exp(tot - cs)
    at_out[...] = (-kk * jnp.exp(cs - lw)).astype(BF16)
    bt_out[...] = (bb * w_inv).astype(BF16)
    kt_out[...] = (k2 * w_inv).astype(BF16)
    rt_out[...] = (r * jnp.exp(cs)).astype(BF16)
    bh_out[...] = (bb * w_rem).astype(BF16)
    kh_out[...] = (k2 * w_rem).astype(BF16)
    v_out[...] = v.astype(BF16)
    g_out[...] = g
    bvg_out[...] = _head_sums(r * k2 * rk_ref[...], ones) * v * g
    for c in range(tm // chunk):
        wc_out[c] = jnp.exp(tot[c * chunk:c * chunk + 1, :])


def _rwkv_prep(h, mu, w0, a0, k_k, k_a, r_k, wwa, gup, ones, seq, tm):
    t, cols = h.shape
    aw = w0.shape[1]
    chunk = RWKV_CHUNK
    row_spec = pl.BlockSpec((tm, aw), lambda i: (i, 0))
    full = lambda arr: pl.BlockSpec(arr.shape, lambda i: (0, 0))
    kern = functools.partial(_rwkv_prep_kernel, blocks_per_seq=seq // tm, aw=aw, chunk=chunk)
    return pl.pallas_call(
        kern,
        grid=(t // tm,),
        in_specs=[pl.BlockSpec((tm, cols), lambda i: (i, 0)),
                  pl.BlockSpec((8, cols), lambda i: (jnp.maximum(i * (tm // 8) - 1, 0), 0)),
                  full(mu), full(w0), full(a0), full(k_k), full(k_a), full(r_k), full(wwa),
                  full(gup), full(ones)],
        out_specs=[row_spec] * 9 + [pl.BlockSpec((tm // chunk, 1, aw), lambda i: (i, 0, 0))],
        out_shape=([jax.ShapeDtypeStruct((t, aw), BF16)] * 7 + [jax.ShapeDtypeStruct((t, aw), F32)] * 2
                   + [jax.ShapeDtypeStruct((t // chunk, 1, aw), F32)]),
        compiler_params=_params("parallel"),
        name="rwkv_prep",
    )(h, h, mu, w0, a0, k_k, k_a, r_k, wwa, gup, ones)


def _rwkv_scan_kernel(at_ref, bt_ref, kt_ref, rt_ref, bh_ref, kh_ref, v_ref, wc_ref, y_ref, st_ref,
                      *, pairs, n):
    c = n
    n2 = 2 * n
    nsub = at_ref.shape[0] // c

    @pl.when(pl.program_id(1) == 0)
    def _():
        st_ref[...] = jnp.zeros_like(st_ref)

    row = lax.broadcasted_iota(jnp.int32, (n2, n2), 0)
    col = lax.broadcasted_iota(jnp.int32, (n2, n2), 1)
    same = (row >= c) == (col >= c)
    strict = same & (row > col)
    incl = same & (row >= col)
    incl2 = jnp.concatenate([incl, incl], axis=1)
    eye = (row == col).astype(F32)
    lo = lax.broadcasted_iota(jnp.int32, (c, n2), 1) < n

    def stack(ref, q, sl):
        x = ref[q * c:(q + 1) * c, sl]
        zero = jnp.zeros_like(x)
        return jnp.concatenate([jnp.where(lo, x, zero), jnp.where(lo, zero, x)], axis=0)

    sls = [slice(p * n2, (p + 1) * n2) for p in range(pairs)]
    units = [(q, p) for q in range(nsub) for p in range(pairs)]
    stacks = {u: [stack(ref, u[0], sls[u[1]])
                  for ref in (at_ref, bt_ref, kt_ref, rt_ref, bh_ref, kh_ref, v_ref)] for u in units}
    ar = {u: jnp.concatenate([stacks[u][0], stacks[u][3]], axis=0) for u in units}
    big = {u: _dot_nt(ar[u], jnp.concatenate([stacks[u][1], stacks[u][2]], axis=0)) for u in units}
    l_ab = {u: jnp.where(strict, big[u][:n2, :n2], 0.0) for u in units}
    l_ak = {u: jnp.where(strict, big[u][:n2, n2:], 0.0).astype(BF16) for u in units}
    m_all = {u: jnp.where(incl2, big[u][n2:, :], 0.0).astype(BF16) for u in units}

    tinv = {u: eye + l_ab[u] for u in units}
    pw = {u: l_ab[u].astype(BF16) for u in units}
    for _ in range(int(math.log2(c)) - 1):
        pw = {u: _dot(pw[u], pw[u]).astype(BF16) for u in units}
        tinv = {u: tinv[u] + _dot(tinv[u].astype(BF16), pw[u]) for u in units}
    tinv = {u: tinv[u].astype(BF16) for u in units}
    lv = {u: _dot(l_ak[u], stacks[u][6]) for u in units}

    state = [st_ref[p] for p in range(pairs)]
    for q in range(nsub):
        us = [(q, p) for p in range(pairs)]
        from_state = [_dot_nt(ar[u], state[u[1]].astype(BF16)) for u in us]
        x1 = [(from_state[p][:n2] + lv[us[p]]).astype(BF16) for p in range(pairs)]
        uu = [_dot(tinv[us[p]], x1[p]).astype(BF16) for p in range(pairs)]
        uv = [jnp.concatenate([uu[p], stacks[us[p]][6]], axis=0) for p in range(pairs)]
        for p in range(pairs):
            u = us[p]
            yst = from_state[p][n2:] + _dot(m_all[u], uv[p])
            state[p] = state[p] * wc_ref[q, :, sls[p]] + _dot_tn(
                uv[p], jnp.concatenate([stacks[u][4], stacks[u][5]], axis=0))
            y_ref[q * c:(q + 1) * c, sls[p]] = yst[:c] + yst[c:]
    for p in range(pairs):
        st_ref[p] = state[p]


def _rwkv_scan(at, bt, kt, rt, bh, kh, v, wc, batch, seq):
    t, aw = at.shape
    c = RWKV_CHUNK
    n = HEAD_DIM
    assert c == n
    pairs = aw // (2 * n)
    nsub = RWKV_CHUNKS_PER_STEP
    nstep = seq // (c * nsub)
    tok = pl.BlockSpec((c * nsub, aw), lambda b, j: (b * nstep + j, 0))
    kern = functools.partial(_rwkv_scan_kernel, pairs=pairs, n=n)
    return pl.pallas_call(
        kern,
        grid=(batch, nstep),
        in_specs=[tok] * 7 + [pl.BlockSpec((nsub, 1, aw), lambda b, j: (b * nstep + j, 0, 0))],
        out_specs=tok,
        out_shape=jax.ShapeDtypeStruct((t, aw), F32),
        scratch_shapes=[pltpu.VMEM((pairs, 2 * n, 2 * n), F32)],
        compiler_params=_params("parallel", "arbitrary"),
        name="rwkv_scan",
    )(at, bt, kt, rt, bh, kh, v, wc)


def _rope_tables(positions, head_dim, lanes):
    rd = head_dim // ROPE_FRACTION
    half = rd // 2
    inv_freq = ROPE_THETA ** (-jnp.arange(half, dtype=F32) / half)
    ang = positions.astype(F32).reshape(-1, 1) * inv_freq
    cos, sin = jnp.cos(ang), jnp.sin(ang)
    t = ang.shape[0]
    pad = jnp.zeros((t, head_dim - rd), F32)
    zero = jnp.zeros((t, half), F32)
    c = jnp.concatenate([cos, cos, pad + 1.0], -1)
    sn = jnp.concatenate([-sin, zero, pad], -1)
    sp = jnp.concatenate([zero, sin, pad], -1)
    rep = lanes // head_dim
    return tuple(jnp.tile(x, (1, rep)) for x in (c, sn, sp)), half


def _rope(x, c, sn, sp, half):
    width = x.shape[1]
    rep = width // c.shape[1]
    if rep > 1:
        c, sn, sp = (jnp.concatenate([y] * rep, axis=1) for y in (c, sn, sp))
    nxt = pltpu.roll(x, width - half, axis=1)
    prv = pltpu.roll(x, half, axis=1)
    return x * c + nxt * sn + prv * sp


def _swa_prep_kernel(q_ref, kv_ref, c_ref, sn_ref, sp_ref, q_out, k_out, v_out, *, half, scale):
    c, sn, sp = c_ref[...], sn_ref[...], sp_ref[...]
    q_out[...] = (_rope(q_ref[...], c, sn, sp, half) * scale).astype(q_out.dtype)
    kv = kv_ref[...]
    kw = kv.shape[1] // 2
    k_out[...] = _rope(kv[:, :kw], c, sn, sp, half).astype(k_out.dtype)
    v_out[...] = kv[:, kw:].astype(v_out.dtype)


def _swa_prep(hb, tabs, half, qw, kvw, tm):
    t = hb.shape[0]
    kern = functools.partial(_swa_prep_kernel, half=half, scale=HEAD_DIM ** -0.5)
    tab_spec = pl.BlockSpec((tm, 128), lambda i: (i, 0))
    return pl.pallas_call(
        kern,
        grid=(t // tm,),
        in_specs=[pl.BlockSpec((tm, qw), lambda i: (i, 0)),
                  pl.BlockSpec((tm, 2 * kvw), lambda i: (i, qw // (2 * kvw))),
                  tab_spec, tab_spec, tab_spec],
        out_specs=[pl.BlockSpec((tm, qw), lambda i: (i, 0)),
                   pl.BlockSpec((tm, kvw), lambda i: (i, 0)),
                   pl.BlockSpec((tm, kvw), lambda i: (i, 0))],
        out_shape=[jax.ShapeDtypeStruct((t, qw), BF16),
                   jax.ShapeDtypeStruct((t, kvw), BF16),
                   jax.ShapeDtypeStruct((t, kvw), BF16)],
        compiler_params=_params("parallel"),
        name="swa_prep",
    )(hb, hb, *tabs)


def _swa_kernel(sink_ref, q_ref, kp_ref, kc_ref, vp_ref, vc_ref, o_ref, *, q_heads, kv_heads, n):
    blk = q_ref.shape[0]
    j = pl.program_id(1)
    kw = jnp.concatenate([kp_ref[...], kc_ref[...]], axis=0)
    vw = jnp.concatenate([vp_ref[...], vc_ref[...]], axis=0)
    qi = lax.broadcasted_iota(jnp.int32, (blk, 2 * blk), 0)
    kj = lax.broadcasted_iota(jnp.int32, (blk, 2 * blk), 1)
    rel = qi + blk - kj
    mask = (rel >= 0) & (rel < WINDOW) & ((kj >= blk) | (j > 0))
    group = q_heads // kv_heads
    ss = [_dot_nt(q_ref[:, h * n:(h + 1) * n], kw[:, (h // group) * n:(h // group + 1) * n])
          for h in range(q_heads)]
    ps = []
    for h in range(q_heads):
        s = jnp.where(mask, ss[h], -jnp.inf)
        sink = sink_ref[h]
        m = jnp.maximum(jnp.max(s, -1, keepdims=True), sink)
        e = jnp.exp(s - m)
        den = jnp.sum(e, -1, keepdims=True) + jnp.exp(sink - m)
        ps.append((e / den).astype(vw.dtype))
    for h in range(q_heads):
        g = h // group
        o_ref[:, h * n:(h + 1) * n] = _dot(ps[h], vw[:, g * n:(g + 1) * n]).astype(o_ref.dtype)


def _swa(q, k, v, sinks, batch, seq):
    t, qw = q.shape
    kvw = k.shape[1]
    n = HEAD_DIM
    blk = WINDOW
    nb = seq // blk
    cur = lambda b, j: (b * nb + j, 0)
    prev = lambda b, j: (b * nb + jnp.maximum(j - 1, 0), 0)
    kern = functools.partial(_swa_kernel, q_heads=qw // n, kv_heads=kvw // n, n=n)
    return pl.pallas_call(
        kern,
        grid=(batch, nb),
        in_specs=[pl.BlockSpec(memory_space=pltpu.SMEM),
                  pl.BlockSpec((blk, qw), cur),
                  pl.BlockSpec((blk, kvw), prev), pl.BlockSpec((blk, kvw), cur),
                  pl.BlockSpec((blk, kvw), prev), pl.BlockSpec((blk, kvw), cur)],
        out_specs=pl.BlockSpec((blk, qw), cur),
        out_shape=jax.ShapeDtypeStruct((t, qw), F32),
        compiler_params=_params("parallel", "arbitrary"),
        name="swa",
    )(sinks, q, k, k, v, v)


def _diff_rope_kernel(x_ref, c_ref, sn_ref, sp_ref, o_ref, *, half):
    x = x_ref[...].astype(F32)
    o_ref[...] = _rope(x, c_ref[...], sn_ref[...], sp_ref[...], half).astype(o_ref.dtype)


def _diff_rope(h, tabs, half, width, tm, tn):
    t = h.shape[0]
    tab_spec = pl.BlockSpec((tm, 128), lambda i, j: (i, 0))
    blk = pl.BlockSpec((tm, tn), lambda i, j: (i, j))
    return pl.pallas_call(
        functools.partial(_diff_rope_kernel, half=half),
        grid=(t // tm, 2 * width // tn),
        in_specs=[blk, tab_spec, tab_spec, tab_spec],
        out_specs=blk,
        out_shape=jax.ShapeDtypeStruct((t, 2 * width), h.dtype),
        compiler_params=_params("parallel", "arbitrary"),
        name="diff_rope",
    )(h, *tabs)


def _diff_attn_kernel(sc_ref, q_ref, k_ref, v_ref, g_ref, o_ref, m_ref, l_ref, acc_ref, *, bq, bk, rg,
                      dh):
    qi = pl.program_id(2)
    m_ref[...] = jnp.full_like(m_ref, -jnp.inf)
    l_ref[...] = jnp.zeros_like(l_ref)
    acc_ref[...] = jnp.zeros_like(acc_ref)
    ratio = bq // bk

    def block(j, diag_offset):
        off = pl.multiple_of(j * bk, bk)
        v = v_ref[pl.ds(off, bk), :]
        rows = [slice(r * rg, (r + 1) * rg) for r in range(bq // rg)]
        ss = [[_dot_nt(q_ref[rs, c * dh:(c + 1) * dh], k_ref[pl.ds(off, bk), c * dh:(c + 1) * dh])
               for c in range(2)] for rs in rows]
        ps, alphas = [], []
        for r, rs in enumerate(rows):
            if diag_offset is not None:
                row = lax.broadcasted_iota(jnp.int32, (rg, bk), 0) + r * rg
                col = lax.broadcasted_iota(jnp.int32, (rg, bk), 1) + diag_offset
                ss[r] = [jnp.where(col <= row, s, -jnp.inf) for s in ss[r]]
            pr, ar = [], []
            for c in range(2):
                m_old = m_ref[c, rs, :]
                m_new = jnp.maximum(m_old, jnp.max(ss[r][c], -1, keepdims=True))
                alpha = jnp.exp2(m_old - m_new)
                p = jnp.exp2(ss[r][c] - jnp.concatenate([m_new] * (bk // 128), axis=1))
                l_ref[c, rs, :] = alpha * l_ref[c, rs, :] + jnp.sum(p, -1, keepdims=True)
                m_ref[c, rs, :] = m_new
                pr.append(p.astype(v.dtype))
                ar.append(jnp.concatenate([alpha] * (2 * dh // 128), axis=1))
            ps.append(pr)
            alphas.append(ar)
        for r, rs in enumerate(rows):
            for c in range(2):
                acc_ref[c, rs, :] = alphas[r][c] * acc_ref[c, rs, :] + _dot(ps[r][c], v)

    def body(j, carry):
        block(j, None)
        return carry

    lax.fori_loop(0, qi * ratio, body, 0)
    for h in range(ratio):
        block(qi * ratio + h, h * bk)

    lam = sc_ref[0]
    post = sc_ref[1]
    rep = 2 * dh // 128
    o1 = acc_ref[0] / jnp.concatenate([l_ref[0]] * rep, axis=1)
    o2 = acc_ref[1] / jnp.concatenate([l_ref[1]] * rep, axis=1)
    out = o1 - lam * o2
    out = out * lax.rsqrt(jnp.mean(out * out, -1, keepdims=True) + SUBLN_EPS)
    o_ref[...] = (out * g_ref[...] * post).astype(o_ref.dtype)


def _diff_attn(qk, hv, scalars, subln_g, batch, seq, bq, bk, rg):
    t = qk.shape[0]
    width = qk.shape[1] // 2
    dh = DIFF_HEAD_DIM
    hw = 2 * dh
    heads = width // hw
    nq = seq // bq
    kern = functools.partial(_diff_attn_kernel, bq=bq, bk=bk, rg=rg, dh=dh)
    resident = lambda first, bufs: pl.BlockSpec((seq, hw), lambda b, h, qi: (b, first + h),
                                                pipeline_mode=pl.Buffered(bufs))
    return pl.pallas_call(
        kern,
        grid=(batch, heads, nq),
        in_specs=[pl.BlockSpec(memory_space=pltpu.SMEM),
                  pl.BlockSpec((bq, hw), lambda b, h, qi: (b * nq + qi, h)),
                  resident(heads, 2), resident(2 * heads, 1),
                  pl.BlockSpec((1, hw), lambda b, h, qi: (0, 0))],
        out_specs=pl.BlockSpec((bq, hw), lambda b, h, qi: (b * nq + qi, h)),
        out_shape=jax.ShapeDtypeStruct((t, width), BF16),
        scratch_shapes=[pltpu.VMEM((2, bq, 128), F32), pltpu.VMEM((2, bq, 128), F32),
                        pltpu.VMEM((2, bq, hw), F32)],
        compiler_params=_params("parallel", "parallel", "arbitrary"),
        name="diff_attn",
    )(scalars, qk, qk, hv, subln_g)


def _rwkv_finish(y, g, bvg, gnw, gnb, ones, n):
    ym = _head_sums(y, ones) * (1.0 / n)
    d = y - ym
    yv = _head_sums(d * d, ones) * (1.0 / n)
    return (d * lax.rsqrt(yv + WKV_GN_EPS) * gnw + gnb) * g + bvg


def _even_outproj_ln_kernel(y_ref, g_ref, bvg_ref, gnw_ref, gnb_ref, ones_ref, yb_ref, wa_ref, wb_ref,
                            x_ref, lg_ref, lb_ref, o_ref, obf_ref, *, alpha, n):
    ya = _rwkv_finish(y_ref[...], g_ref[...], bvg_ref[...], gnw_ref[...], gnb_ref[...],
                      ones_ref[...], n)
    mix = _dot(ya.astype(BF16), wa_ref[...]) + _dot(yb_ref[...].astype(BF16), wb_ref[...])
    out = _layer_norm(alpha * x_ref[...] + mix, lg_ref[...], lb_ref[...])
    o_ref[...] = out
    obf_ref[...] = out.astype(BF16)


def _even_outproj_ln(y, g, bvg, gn_w, gn_b, ones, yb, wa, wb, x, lg, lb, alpha, tm):
    t, d = x.shape
    kern = functools.partial(_even_outproj_ln_kernel, alpha=alpha, n=HEAD_DIM)
    row = lambda arr: pl.BlockSpec((tm, arr.shape[1]), lambda i: (i, 0))
    full = lambda arr: pl.BlockSpec(arr.shape, lambda i: (0, 0), pipeline_mode=pl.Buffered(1))
    return pl.pallas_call(
        kern,
        grid=(t // tm,),
        in_specs=[row(y), row(g), row(bvg), full(gn_w), full(gn_b), full(ones), row(yb), full(wa),
                  full(wb), row(x), full(lg), full(lb)],
        out_specs=[row(x), row(x)],
        out_shape=[jax.ShapeDtypeStruct((t, d), F32), jax.ShapeDtypeStruct((t, d), BF16)],
        compiler_params=_params("parallel"),
        name="even_outproj_ln",
    )(y, g, bvg, gn_w, gn_b, ones, yb, wa, wb, x, lg, lb)


def _outproj_ln_kernel(*refs, n_in, alpha):
    ys = refs[:n_in]
    ws = refs[n_in:2 * n_in]
    x_ref, g_ref, b_ref, o_ref, obf_ref = refs[2 * n_in:]
    mix = _dot(ys[0][...].astype(BF16), ws[0][...])
    for y, w in zip(ys[1:], ws[1:]):
        mix = mix + _dot(y[...].astype(BF16), w[...])
    out = _layer_norm(alpha * x_ref[...] + mix, g_ref[...], b_ref[...])
    o_ref[...] = out
    obf_ref[...] = out.astype(BF16)


def _outproj_ln(ys, ws, x, g, b, alpha, tm):
    t, d = x.shape
    n_in = len(ys)
    kern = functools.partial(_outproj_ln_kernel, n_in=n_in, alpha=alpha)
    row = lambda width: pl.BlockSpec((tm, width), lambda i: (i, 0))
    full = lambda arr: pl.BlockSpec(arr.shape, lambda i: (0, 0), pipeline_mode=pl.Buffered(1))
    return pl.pallas_call(
        kern,
        grid=(t // tm,),
        in_specs=[row(y.shape[1]) for y in ys] + [full(w) for w in ws] + [row(d), full(g), full(b)],
        out_specs=[row(d), row(d)],
        out_shape=[jax.ShapeDtypeStruct((t, d), F32), jax.ShapeDtypeStruct((t, d), BF16)],
        compiler_params=_params("parallel"),
        name="outproj_ln",
    )(*ys, *ws, x, g, b)


def _ffn_up_kernel(x_ref, w1_ref, w3_ref, o_ref):
    x = x_ref[...]
    h1 = _dot(x, w1_ref[...])
    h3 = _dot(x, w3_ref[...])
    o_ref[...] = (h1 * jax.nn.sigmoid(h1) * h3).astype(o_ref.dtype)


def _ffn_up(x, w1, w3, tm, tf):
    t, d = x.shape
    f = w1.shape[1]
    return pl.pallas_call(
        _ffn_up_kernel,
        grid=(t // tm, f // tf),
        in_specs=[pl.BlockSpec((tm, d), lambda i, j: (i, 0)),
                  pl.BlockSpec((d, tf), lambda i, j: (0, j)),
                  pl.BlockSpec((d, tf), lambda i, j: (0, j))],
        out_specs=pl.BlockSpec((tm, tf), lambda i, j: (i, j)),
        out_shape=jax.ShapeDtypeStruct((t, f), BF16),
        compiler_params=_params("parallel", "arbitrary"),
        name="ffn_up",
    )(x, w1, w3)


def _ffn_down_ple_kernel(a_ref, w2_ref, x_ref, g_ref, b_ref, p_ref, wg_ref, wp_ref, o_ref, obf_ref,
                         *, alpha):
    y = _dot(a_ref[...], w2_ref[...])
    x2 = _layer_norm(alpha * x_ref[...] + y, g_ref[...], b_ref[...])
    gate = jax.nn.sigmoid(_dot(x2.astype(BF16), wg_ref[...]))
    proj = _dot(p_ref[...].astype(BF16), wp_ref[...])
    out = x2 + gate * proj
    o_ref[...] = out
    obf_ref[...] = out.astype(BF16)


def _ffn_down_ple(act, w2, x, g, b, p, layer, wg, wp, alpha, tm):
    t, d = x.shape
    f = act.shape[1]
    pd = p.shape[1]
    p_first = layer * (t // tm)
    kern = functools.partial(_ffn_down_ple_kernel, alpha=alpha)
    row = lambda width: pl.BlockSpec((tm, width), lambda i: (i, 0))
    once = lambda arr: pl.BlockSpec(arr.shape, lambda i: (0, 0), pipeline_mode=pl.Buffered(1))
    return pl.pallas_call(
        kern,
        grid=(t // tm,),
        in_specs=[row(f), once(w2), row(d), once(g), once(b),
                  pl.BlockSpec((tm, pd), lambda i: (p_first + i, 0)), once(wg), once(wp)],
        out_specs=[row(d), row(d)],
        out_shape=[jax.ShapeDtypeStruct((t, d), F32), jax.ShapeDtypeStruct((t, d), BF16)],
        compiler_params=_params("parallel"),
        name="ffn_down_ple",
    )(act, w2, x, g, b, p, wg, wp)


def _even_mixer(xbf, positions, w_in, mu, w0, w_up, a0, a_up, g_up, k_k, k_a, r_k, gn_w, gn_b,
                sinks, batch, seq):
    aw = w0.shape[0]
    dl = w_up.shape[0]
    al = a_up.shape[0]
    gl = g_up.shape[0]
    a_cols = 3 * aw + dl + al + gl
    gl_pad = 384
    assert dl == 64 and al == 64 and gl <= gl_pad
    d = w_in.shape[0]
    pad_cols = jnp.zeros((d, gl_pad - gl), F32)
    w_a = jnp.concatenate([w_in[:, :a_cols], pad_cols], axis=1).astype(BF16)
    w_b = w_in[:, a_cols:].astype(BF16)
    mu_p = jnp.concatenate([mu, jnp.zeros((gl_pad - gl,), F32)])[None, :]
    zeros = jnp.zeros((dl, aw), F32)
    wwa = jnp.concatenate([jnp.concatenate([w_up, zeros], 1), jnp.concatenate([zeros, a_up], 1)], 0)
    gup = jnp.concatenate([g_up, jnp.zeros((gl_pad - gl, aw), F32)], 0)

    h_a = _matmul(xbf, w_a, F32, 2048, 512, "proj_rwkv")
    h_b = _matmul(xbf, w_b, F32, 1024, w_b.shape[1], "proj_swa")

    ones = jnp.kron(jnp.eye(2, dtype=F32), jnp.ones((HEAD_DIM, HEAD_DIM), F32)).astype(BF16)
    at, bt, kt, rt, bh, kh, v, g, bvg, wc = _rwkv_prep(
        h_a, mu_p, w0[None], a0[None], k_k[None], k_a[None], r_k.reshape(1, aw), wwa, gup, ones,
        seq, 256)
    y = _rwkv_scan(at, bt, kt, rt, bh, kh, v, wc, batch, seq)

    qw = sinks.shape[0] * HEAD_DIM
    kvw = (w_b.shape[1] - qw) // 2
    tabs, half = _rope_tables(positions, HEAD_DIM, 128)
    q, kb, vb = _swa_prep(h_b, tabs, half, qw, kvw, 512)
    yb = _swa(q, kb, vb, sinks, batch, seq)
    return (y, g, bvg, gn_w[None], gn_b[None], ones), yb


def _odd_mixer(xbf, positions, w_in, lam_p, subln_g, layer_idx, batch, seq):
    width = w_in.shape[1] // 3
    q_scale = DIFF_HEAD_DIM ** -0.5 * math.log2(math.e)
    w = jnp.concatenate([w_in[:, :width] * q_scale, w_in[:, width:]], axis=1).astype(BF16)
    h = _matmul(xbf, w, BF16, 2048, 1024, "proj_diff")
    tabs, half = _rope_tables(positions, DIFF_HEAD_DIM, 128)
    qk = _diff_rope(h, tabs, half, width, 512, 1024)
    lam_init = 0.8 - 0.6 * math.exp(-0.3 * layer_idx)
    lv = lam_p.astype(F32)
    lam = jnp.exp(jnp.sum(lv[0] * lv[1])) - jnp.exp(jnp.sum(lv[2] * lv[3])) + lam_init
    scalars = jnp.stack([lam, jnp.asarray(1.0 - lam_init, F32)]).astype(F32)
    return _diff_attn(qk, h, scalars, subln_g[None], batch, seq, 1024, 1024, 256)


def kernel(x, p, positions, e_w_in, e_mu, e_w0, e_w_up, e_a0, e_a_up, e_g_up, e_k_k, e_k_a, e_r_k,
           e_gn_w, e_gn_b, e_sinks, e_w_out, o_w_in, o_lambda, o_subln_g, o_w_out,
           ln1_g, ln1_b, ln2_g, ln2_b, ffn_w1, ffn_w3, ffn_w2, ple_w_proj, ple_w_gate):
    batch, seq, d = x.shape
    depth = p.shape[0]
    t = batch * seq
    alpha = (2.0 * depth) ** 0.25
    xf = x.reshape(t, d)
    xbf = xf.astype(BF16)
    p_rows = p.reshape(depth * t, p.shape[-1])
    for i in range(depth):
        j = i // 2
        if i % 2 == 0:
            ya, yb = _even_mixer(xbf, positions, e_w_in[j], e_mu[j], e_w0[j], e_w_up[j], e_a0[j],
                                 e_a_up[j], e_g_up[j], e_k_k[j], e_k_a[j], e_r_k[j], e_gn_w[j],
                                 e_gn_b[j], e_sinks[j], batch, seq)
            aw = ya[0].shape[1]
            w_out = e_w_out[j].astype(BF16)
            xf, xbf = _even_outproj_ln(*ya, yb, w_out[:aw], w_out[aw:], xf, ln1_g[i][None],
                                       ln1_b[i][None], alpha, 512)
        else:
            y = _odd_mixer(xbf, positions, o_w_in[j], o_lambda[j], o_subln_g[j], i, batch, seq)
            xf, xbf = _outproj_ln([y], [o_w_out[j].astype(BF16)], xf, ln1_g[i][None],
                                  ln1_b[i][None], alpha, 512)
        act = _ffn_up(xbf, ffn_w1[i].astype(BF16), ffn_w3[i].astype(BF16), 1024, 512)
        xf, xbf = _ffn_down_ple(act, ffn_w2[i].astype(BF16), xf, ln2_g[i][None], ln2_b[i][None],
                                p_rows, i, ple_w_gate[i].astype(BF16),
                                ple_w_proj[i].astype(BF16), alpha, 256)
    return xf.reshape(batch, seq, d)
```

```python
import functools
import math

import jax
import jax.numpy as jnp
from jax import lax
from jax.experimental import pallas as pl
from jax.experimental.pallas import tpu as pltpu

F32 = jnp.float32
BF16 = jnp.bfloat16
HI = lax.Precision.HIGHEST

HEAD_DIM = 64
DIFF_HEAD_DIM = 128
WINDOW = 128
ROPE_THETA = 500000.0
ROPE_FRACTION = 4
LN_EPS = 1e-5
WKV_GN_EPS = 64e-5
SUBLN_EPS = 1e-5
RWKV_CHUNK = 64
RWKV_CHUNKS_PER_STEP = 4
VMEM_LIMIT = 56 * 1024 * 1024


def _dot(a, b, prec=None):
    return jnp.dot(a, b, preferred_element_type=F32, precision=prec)


def _dot_nt(a, b, prec=None):
    return lax.dot_general(a, b, (((1,), (1,)), ((), ())), preferred_element_type=F32, precision=prec)


def _dot_tn(a, b, prec=None):
    return lax.dot_general(a, b, (((0,), (0,)), ((), ())), preferred_element_type=F32, precision=prec)


def _params(*sem):
    return pltpu.CompilerParams(dimension_semantics=sem, vmem_limit_bytes=VMEM_LIMIT)


def _layer_norm(z, g, b):
    mu = jnp.mean(z, -1, keepdims=True)
    d = z - mu
    var = jnp.mean(d * d, -1, keepdims=True)
    return d * lax.rsqrt(var + LN_EPS) * g + b


def _mm_kernel(x_ref, w_ref, o_ref):
    o_ref[...] = _dot(x_ref[...], w_ref[...]).astype(o_ref.dtype)


def _matmul(x, w, out_dtype, tm, tn, name):
    t, k = x.shape
    n = w.shape[1]
    return pl.pallas_call(
        _mm_kernel,
        grid=(t // tm, n // tn),
        in_specs=[pl.BlockSpec((tm, k), lambda i, j: (i, 0)),
                  pl.BlockSpec((k, tn), lambda i, j: (0, j))],
        out_specs=pl.BlockSpec((tm, tn), lambda i, j: (i, j)),
        out_shape=jax.ShapeDtypeStruct((t, n), out_dtype),
        compiler_params=_params("parallel", "arbitrary"),
        name=name,
    )(x, w)


def _bf16_parts(x, n):
    parts = []
    for _ in range(n):
        p = x.astype(BF16)
        parts.append(p)
        x = x - p.astype(F32)
    return parts


def _split_dot(x, w, n=2):
    return sum(_dot(p, w) for p in _bf16_parts(x, n))


def _dot3(a, b):
    ah, al = _bf16_parts(a, 2)
    bh, bl = _bf16_parts(b, 2)
    return _dot(ah, bh) + _dot(ah, bl) + _dot(al, bh)


def _head_sums(x, ones):
    gw = ones.shape[0]
    parts = [_split_dot(x[:, i:i + gw], ones) for i in range(0, x.shape[1], gw)]
    return jnp.concatenate(parts, axis=1)


def _rwkv_prep_kernel(h_ref, hp_ref, mu_ref, w0_ref, a0_ref, kk_ref, ka_ref, rk_ref, wwa_ref,
                      gup_ref, ones_ref, at_out, bt_out, kt_out, rt_out, bh_out, kh_out, v_out,
                      g_out, bvg_out, wc_out, *, blocks_per_seq, aw, chunk):
    i = pl.program_id(0)
    h = h_ref[...]
    tm = h.shape[0]
    first = (i % blocks_per_seq) == 0
    prev_row = jnp.where(first, 0.0, hp_ref[7:8, :])
    shifted = pltpu.roll(h, 1, axis=0)
    row = lax.broadcasted_iota(jnp.int32, h.shape, 0)
    shifted = jnp.where(row == 0, prev_row, shifted)
    hm = h + (shifted - h) * mu_ref[...]
    r = hm[:, 0:aw]
    k = hm[:, aw:2 * aw]
    v = hm[:, 2 * aw:3 * aw]
    z = hm[:, 3 * aw:3 * aw + 128]
    lane = lax.broadcasted_iota(jnp.int32, z.shape, 1)
    z = jnp.where(lane < 64, jnp.tanh(z), z)
    wa = _dot3(z, wwa_ref[...])
    w = -jax.nn.softplus(-(w0_ref[...] + wa[:, :aw])) - 0.5
    a = jax.nn.sigmoid(a0_ref[...] + wa[:, aw:])
    lg = hm[:, 3 * aw + 128:]
    g = _dot3(jax.nn.sigmoid(lg), gup_ref[...])
    lw = -jnp.exp(w)

    ri = lax.broadcasted_iota(jnp.int32, (tm, tm), 0)
    ci = lax.broadcasted_iota(jnp.int32, (tm, tm), 1)
    shift = int(math.log2(chunk))
    same = (ri >> shift) == (ci >> shift)
    lw_parts = _bf16_parts(lw, 3)
    tri = jnp.where(same & (ri >= ci), 1.0, 0.0).astype(BF16)
    blk = jnp.where(same, 1.0, 0.0).astype(BF16)
    cs = sum(_dot(tri, part) for part in lw_parts)
    tot = sum(_dot(blk, part) for part in lw_parts)

    ones = ones_ref[...]
    kkr = k * kk_ref[...]
    kk = kkr / jnp.maximum(jnp.sqrt(_head_sums(kkr * kkr, ones)), 1e-12)
    k2 = k * (1.0 + (a - 1.0) * ka_ref[...])
    bb = kk * a
    w_inv = jnp.exp(-cs)
    w_rem = jnp.exp(tot - cs)
    at_out[...] = (-kk * jnp.exp(cs - lw)).astype(BF16)
    bt_out[...] = (bb * w_inv).astype(BF16)
    kt_out[...] = (k2 * w_inv).astype(BF16)
    rt_out[...] = (r * jnp.exp(cs)).astype(BF16)
    bh_out[...] = (bb * w_rem).astype(BF16)
    kh_out[...] = (k2 * w_rem).astype(BF16)
    v_out[...] = v.astype(BF16)
    g_out[...] = g
    bvg_out[...] = _head_sums(r * k2 * rk_ref[...], ones) * v * g
    for c in range(tm // chunk):
        wc_out[c] = jnp.exp(tot[c * chunk:c * chunk + 1, :])


def _rwkv_prep(h, mu, w0, a0, k_k, k_a, r_k, wwa, gup, ones, seq, tm):
    t, cols = h.shape
    aw = w0.shape[1]
    chunk = RWKV_CHUNK
    row_spec = pl.BlockSpec((tm, aw), lambda i: (i, 0))
    full = lambda arr: pl.BlockSpec(arr.shape, lambda i: (0, 0))
    kern = functools.partial(_rwkv_prep_kernel, blocks_per_seq=seq // tm, aw=aw, chunk=chunk)
    return pl.pallas_call(
        kern,
        grid=(t // tm,),
        in_specs=[pl.BlockSpec((tm, cols), lambda i: (i, 0)),
                  pl.BlockSpec((8, cols), lambda i: (jnp.maximum(i * (tm // 8) - 1, 0), 0)),
                  full(mu), full(w0), full(a0), full(k_k), full(k_a), full(r_k), full(wwa),
                  full(gup), full(ones)],
        out_specs=[row_spec] * 9 + [pl.BlockSpec((tm // chunk, 1, aw), lambda i: (i, 0, 0))],
        out_shape=([jax.ShapeDtypeStruct((t, aw), BF16)] * 7 + [jax.ShapeDtypeStruct((t, aw), F32)] * 2
                   + [jax.ShapeDtypeStruct((t // chunk, 1, aw), F32)]),
        compiler_params=_params("parallel"),
        name="rwkv_prep",
    )(h, h, mu, w0, a0, k_k, k_a, r_k, wwa, gup, ones)


def _rwkv_scan_kernel(at_ref, bt_ref, kt_ref, rt_ref, bh_ref, kh_ref, v_ref, wc_ref, y_ref, st_ref,
                      *, pairs, n):
    c = n
    n2 = 2 * n
    nsub = at_ref.shape[0] // c

    @pl.when(pl.program_id(1) == 0)
    def _():
        st_ref[...] = jnp.zeros_like(st_ref)

    row = lax.broadcasted_iota(jnp.int32, (n2, n2), 0)
    col = lax.broadcasted_iota(jnp.int32, (n2, n2), 1)
    same = (row >= c) == (col >= c)
    strict = same & (row > col)
    incl = same & (row >= col)
    incl2 = jnp.concatenate([incl, incl], axis=1)
    eye = (row == col).astype(F32)
    lo = lax.broadcasted_iota(jnp.int32, (c, n2), 1) < n

    def stack(ref, q, sl):
        x = ref[q * c:(q + 1) * c, sl]
        zero = jnp.zeros_like(x)
        return jnp.concatenate([jnp.where(lo, x, zero), jnp.where(lo, zero, x)], axis=0)

    sls = [slice(p * n2, (p + 1) * n2) for p in range(pairs)]
    units = [(q, p) for q in range(nsub) for p in range(pairs)]
    stacks = {u: [stack(ref, u[0], sls[u[1]])
                  for ref in (at_ref, bt_ref, kt_ref, rt_ref, bh_ref, kh_ref, v_ref)] for u in units}
    ar = {u: jnp.concatenate([stacks[u][0], stacks[u][3]], axis=0) for u in units}
    big = {u: _dot_nt(ar[u], jnp.concatenate([stacks[u][1], stacks[u][2]], axis=0)) for u in units}
    l_ab = {u: jnp.where(strict, big[u][:n2, :n2], 0.0) for u in units}
    l_ak = {u: jnp.where(strict, big[u][:n2, n2:], 0.0).astype(BF16) for u in units}
    m_all = {u: jnp.where(incl2, big[u][n2:, :], 0.0).astype(BF16) for u in units}

    tinv = {u: eye + l_ab[u] for u in units}
    pw = {u: l_ab[u].astype(BF16) for u in units}
    for _ in range(int(math.log2(c)) - 1):
        pw = {u: _dot(pw[u], pw[u]).astype(BF16) for u in units}
        tinv = {u: tinv[u] + _dot(tinv[u].astype(BF16), pw[u]) for u in units}
    tinv = {u: tinv[u].astype(BF16) for u in units}
    lv = {u: _dot(l_ak[u], stacks[u][6]) for u in units}

    state = [st_ref[p] for p in range(pairs)]
    for q in range(nsub):
        us = [(q, p) for p in range(pairs)]
        from_state = [_dot_nt(ar[u], state[u[1]].astype(BF16)) for u in us]
        x1 = [(from_state[p][:n2] + lv[us[p]]).astype(BF16) for p in range(pairs)]
        uu = [_dot(tinv[us[p]], x1[p]).astype(BF16) for p in range(pairs)]
        uv = [jnp.concatenate([uu[p], stacks[us[p]][6]], axis=0) for p in range(pairs)]
        for p in range(pairs):
            u = us[p]
            yst = from_state[p][n2:] + _dot(m_all[u], uv[p])
            state[p] = state[p] * wc_ref[q, :, sls[p]] + _dot_tn(
                uv[p], jnp.concatenate([stacks[u][4], stacks[u][5]], axis=0))
            y_ref[q * c:(q + 1) * c, sls[p]] = yst[:c] + yst[c:]
    for p in range(pairs):
        st_ref[p] = state[p]


def _rwkv_scan(at, bt, kt, rt, bh, kh, v, wc, batch, seq):
    t, aw = at.shape
    c = RWKV_CHUNK
    n = HEAD_DIM
    assert c == n
    pairs = aw // (2 * n)
    nsub = RWKV_CHUNKS_PER_STEP
    nstep = seq // (c * nsub)
    tok = pl.BlockSpec((c * nsub, aw), lambda b, j: (b * nstep + j, 0))
    kern = functools.partial(_rwkv_scan_kernel, pairs=pairs, n=n)
    return pl.pallas_call(
        kern,
        grid=(batch, nstep),
        in_specs=[tok] * 7 + [pl.BlockSpec((nsub, 1, aw), lambda b, j: (b * nstep + j, 0, 0))],
        out_specs=tok,
        out_shape=jax.ShapeDtypeStruct((t, aw), F32),
        scratch_shapes=[pltpu.VMEM((pairs, 2 * n, 2 * n), F32)],
        compiler_params=_params("parallel", "arbitrary"),
        name="rwkv_scan",
    )(at, bt, kt, rt, bh, kh, v, wc)


def _rope_tables(positions, head_dim, lanes):
    rd = head_dim // ROPE_FRACTION
    half = rd // 2
    inv_freq = ROPE_THETA ** (-jnp.arange(half, dtype=F32) / half)
    ang = positions.astype(F32).reshape(-1, 1) * inv_freq
    cos, sin = jnp.cos(ang), jnp.sin(ang)
    t = ang.shape[0]
    pad = jnp.zeros((t, head_dim - rd), F32)
    zero = jnp.zeros((t, half), F32)
    c = jnp.concatenate([cos, cos, pad + 1.0], -1)
    sn = jnp.concatenate([-sin, zero, pad], -1)
    sp = jnp.concatenate([zero, sin, pad], -1)
    rep = lanes // head_dim
    return tuple(jnp.tile(x, (1, rep)) for x in (c, sn, sp)), half


def _rope(x, c, sn, sp, half):
    width = x.shape[1]
    rep = width // c.shape[1]
    if rep > 1:
        c, sn, sp = (jnp.concatenate([y] * rep, axis=1) for y in (c, sn, sp))
    nxt = pltpu.roll(x, width - half, axis=1)
    prv = pltpu.roll(x, half, axis=1)
    return x * c + nxt * sn + prv * sp


def _swa_prep_kernel(q_ref, kv_ref, c_ref, sn_ref, sp_ref, q_out, k_out, v_out, *, half, scale):
    c, sn, sp = c_ref[...], sn_ref[...], sp_ref[...]
    q_out[...] = (_rope(q_ref[...], c, sn, sp, half) * scale).astype(q_out.dtype)
    kv = kv_ref[...]
    kw = kv.shape[1] // 2
    k_out[...] = _rope(kv[:, :kw], c, sn, sp, half).astype(k_out.dtype)
    v_out[...] = kv[:, kw:].astype(v_out.dtype)


def _swa_prep(hb, tabs, half, qw, kvw, tm):
    t = hb.shape[0]
    kern = functools.partial(_swa_prep_kernel, half=half, scale=HEAD_DIM ** -0.5)
    tab_spec = pl.BlockSpec((tm, 128), lambda i: (i, 0))
    return pl.pallas_call(
        kern,
        grid=(t // tm,),
        in_specs=[pl.BlockSpec((tm, qw), lambda i: (i, 0)),
                  pl.BlockSpec((tm, 2 * kvw), lambda i: (i, qw // (2 * kvw))),
                  tab_spec, tab_spec, tab_spec],
        out_specs=[pl.BlockSpec((tm, qw), lambda i: (i, 0)),
                   pl.BlockSpec((tm, kvw), lambda i: (i, 0)),
                   pl.BlockSpec((tm, kvw), lambda i: (i, 0))],
        out_shape=[jax.ShapeDtypeStruct((t, qw), BF16),
                   jax.ShapeDtypeStruct((t, kvw), BF16),
                   jax.ShapeDtypeStruct((t, kvw), BF16)],
        compiler_params=_params("parallel"),
        name="swa_prep",
    )(hb, hb, *tabs)


def _swa_kernel(sink_ref, q_ref, kp_ref, kc_ref, vp_ref, vc_ref, o_ref, *, q_heads, kv_heads, n):
    blk = q_ref.shape[0]
    j = pl.program_id(1)
    kw = jnp.concatenate([kp_ref[...], kc_ref[...]], axis=0)
    vw = jnp.concatenate([vp_ref[...], vc_ref[...]], axis=0)
    qi = lax.broadcasted_iota(jnp.int32, (blk, 2 * blk), 0)
    kj = lax.broadcasted_iota(jnp.int32, (blk, 2 * blk), 1)
    rel = qi + blk - kj
    mask = (rel >= 0) & (rel < WINDOW) & ((kj >= blk) | (j > 0))
    group = q_heads // kv_heads
    ss = [_dot_nt(q_ref[:, h * n:(h + 1) * n], kw[:, (h // group) * n:(h // group + 1) * n])
          for h in range(q_heads)]
    ps = []
    for h in range(q_heads):
        s = jnp.where(mask, ss[h], -jnp.inf)
        sink = sink_ref[h]
        m = jnp.maximum(jnp.max(s, -1, keepdims=True), sink)
        e = jnp.exp(s - m)
        den = jnp.sum(e, -1, keepdims=True) + jnp.exp(sink - m)
        ps.append((e / den).astype(vw.dtype))
    for h in range(q_heads):
        g = h // group
        o_ref[:, h * n:(h + 1) * n] = _dot(ps[h], vw[:, g * n:(g + 1) * n]).astype(o_ref.dtype)


def _swa(q, k, v, sinks, batch, seq):
    t, qw = q.shape
    kvw = k.shape[1]
    n = HEAD_DIM
    blk = WINDOW
    nb = seq // blk
    cur = lambda b, j: (b * nb + j, 0)
    prev = lambda b, j: (b * nb + jnp.maximum(j - 1, 0), 0)
    kern = functools.partial(_swa_kernel, q_heads=qw // n, kv_heads=kvw // n, n=n)
    return pl.pallas_call(
        kern,
        grid=(batch, nb),
        in_specs=[pl.BlockSpec(memory_space=pltpu.SMEM),
                  pl.BlockSpec((blk, qw), cur),
                  pl.BlockSpec((blk, kvw), prev), pl.BlockSpec((blk, kvw), cur),
                  pl.BlockSpec((blk, kvw), prev), pl.BlockSpec((blk, kvw), cur)],
        out_specs=pl.BlockSpec((blk, qw), cur),
        out_shape=jax.ShapeDtypeStruct((t, qw), F32),
        compiler_params=_params("parallel", "arbitrary"),
        name="swa",
    )(sinks, q, k, k, v, v)


def _diff_rope_kernel(x_ref, c_ref, sn_ref, sp_ref, o_ref, *, half):
    x = x_ref[...].astype(F32)
    o_ref[...] = _rope(x, c_ref[...], sn_ref[...], sp_ref[...], half).astype(o_ref.dtype)


def _diff_rope_k(h, tabs, half, width, tm, tn):
    t = h.shape[0]
    first = width // tn
    tab_spec = pl.BlockSpec((tm, 128), lambda i, j: (i, 0))
    return pl.pallas_call(
        functools.partial(_diff_rope_kernel, half=half),
        grid=(t // tm, width // tn),
        in_specs=[pl.BlockSpec((tm, tn), lambda i, j: (i, first + j)), tab_spec, tab_spec, tab_spec],
        out_specs=pl.BlockSpec((tm, tn), lambda i, j: (i, j)),
        out_shape=jax.ShapeDtypeStruct((t, width), h.dtype),
        compiler_params=_params("parallel", "arbitrary"),
        name="diff_rope_k",
    )(h, *tabs)


def _diff_attn_kernel(sc_ref, q_ref, c_ref, sn_ref, sp_ref, k_ref, v_ref, g_ref, o_ref, m_ref, l_ref,
                      acc_ref, qs_ref, *, blk, rg, dh, half):
    qi = pl.program_id(2)
    qs_ref[...] = _rope(q_ref[...].astype(F32), c_ref[...], sn_ref[...], sp_ref[...],
                        half).astype(qs_ref.dtype)
    groups = blk // rg
    vrep = 2 * dh // 128

    def block(j, diagonal, first):
        off = pl.multiple_of(j * blk, blk)
        nkeys = [(r + 1) * rg if diagonal else blk for r in range(groups)]
        rows = [slice(r * rg, (r + 1) * rg) for r in range(groups)]
        ss = [[_dot_nt(qs_ref[rows[r], c * dh:(c + 1) * dh],
                       k_ref[pl.ds(off, nkeys[r]), c * dh:(c + 1) * dh])
               for c in range(2)] for r in range(groups)]
        ps, alphas = [], []
        for r, rs in enumerate(rows):
            if diagonal:
                row = lax.broadcasted_iota(jnp.int32, (rg, nkeys[r]), 0) + r * rg
                col = lax.broadcasted_iota(jnp.int32, (rg, nkeys[r]), 1)
                ss[r] = [jnp.where(col <= row, s, -jnp.inf) for s in ss[r]]
            pr, ar = [], []
            for c in range(2):
                top = jnp.max(ss[r][c], -1, keepdims=True)
                if first:
                    m_new = jnp.broadcast_to(top, (rg, 128))
                else:
                    m_old = m_ref[c, rs, :]
                    m_new = jnp.maximum(m_old, top)
                    alpha = jnp.exp2(m_old - m_new)
                p = jnp.exp2(ss[r][c] - jnp.concatenate([m_new] * (nkeys[r] // 128), axis=1))
                psum = jnp.sum(p, -1, keepdims=True)
                if first:
                    l_ref[c, rs, :] = jnp.broadcast_to(psum, (rg, 128))
                else:
                    l_ref[c, rs, :] = alpha * l_ref[c, rs, :] + psum
                    ar.append(jnp.concatenate([alpha] * vrep, axis=1))
                m_ref[c, rs, :] = m_new
                pr.append(p.astype(v_ref.dtype))
            ps.append(pr)
            alphas.append(ar)
        v_all = v_ref[pl.ds(off, blk), :]
        for r, rs in enumerate(rows):
            v = v_all[:nkeys[r]]
            for c in range(2):
                pv = _dot(ps[r][c], v)
                acc_ref[c, rs, :] = pv if first else alphas[r][c] * acc_ref[c, rs, :] + pv

    @pl.when(qi == 0)
    def _():
        block(0, True, True)

    @pl.when(qi > 0)
    def _():
        block(0, False, True)

        def body(j, carry):
            block(j, False, False)
            return carry

        lax.fori_loop(1, qi, body, 0)
        block(qi, True, False)

    lam = sc_ref[0]
    post = sc_ref[1]
    o1 = acc_ref[0] / jnp.concatenate([l_ref[0]] * vrep, axis=1)
    o2 = acc_ref[1] / jnp.concatenate([l_ref[1]] * vrep, axis=1)
    out = o1 - lam * o2
    out = out * lax.rsqrt(jnp.mean(out * out, -1, keepdims=True) + SUBLN_EPS)
    o_ref[...] = (out * g_ref[...] * post).astype(o_ref.dtype)


def _diff_attn(h, k_rot, tabs, half, scalars, subln_g, batch, seq, blk, rg):
    t, width = k_rot.shape
    dh = DIFF_HEAD_DIM
    hw = 2 * dh
    heads = width // hw
    nq = seq // blk
    kern = functools.partial(_diff_attn_kernel, blk=blk, rg=rg, dh=dh, half=half)
    resident = lambda first, bufs: pl.BlockSpec((seq, hw), lambda b, h, qi: (b, first + h),
                                                pipeline_mode=pl.Buffered(bufs))
    tab_spec = pl.BlockSpec((blk, 128), lambda b, h, qi: (b * nq + qi, 0))
    return pl.pallas_call(
        kern,
        grid=(batch, heads, nq),
        in_specs=[pl.BlockSpec(memory_space=pltpu.SMEM),
                  pl.BlockSpec((blk, hw), lambda b, h, qi: (b * nq + qi, h)),
                  tab_spec, tab_spec, tab_spec,
                  resident(0, 2), resident(2 * heads, 1),
                  pl.BlockSpec((1, hw), lambda b, h, qi: (0, 0))],
        out_specs=pl.BlockSpec((blk, hw), lambda b, h, qi: (b * nq + qi, h)),
        out_shape=jax.ShapeDtypeStruct((t, width), BF16),
        scratch_shapes=[pltpu.VMEM((2, blk, 128), F32), pltpu.VMEM((2, blk, 128), F32),
                        pltpu.VMEM((2, blk, hw), F32), pltpu.VMEM((blk, hw), BF16)],
        compiler_params=_params("parallel", "parallel", "arbitrary"),
        name="diff_attn",
    )(scalars, h, *tabs, k_rot, h, subln_g)


def _rwkv_finish(y, g, bvg, gnw, gnb, ones, n):
    ym = _head_sums(y, ones) * (1.0 / n)
    d = y - ym
    yv = _head_sums(d * d, ones) * (1.0 / n)
    return (d * lax.rsqrt(yv + WKV_GN_EPS) * gnw + gnb) * g + bvg


def _even_outproj_ln_kernel(y_ref, g_ref, bvg_ref, gnw_ref, gnb_ref, ones_ref, yb_ref, wa_ref, wb_ref,
                            x_ref, lg_ref, lb_ref, o_ref, obf_ref, *, alpha, n):
    ya = _rwkv_finish(y_ref[...], g_ref[...], bvg_ref[...], gnw_ref[...], gnb_ref[...],
                      ones_ref[...], n)
    mix = _dot(ya.astype(BF16), wa_ref[...]) + _dot(yb_ref[...].astype(BF16), wb_ref[...])
    out = _layer_norm(alpha * x_ref[...] + mix, lg_ref[...], lb_ref[...])
    o_ref[...] = out
    obf_ref[...] = out.astype(BF16)


def _even_outproj_ln(y, g, bvg, gn_w, gn_b, ones, yb, wa, wb, x, lg, lb, alpha, tm):
    t, d = x.shape
    kern = functools.partial(_even_outproj_ln_kernel, alpha=alpha, n=HEAD_DIM)
    row = lambda arr: pl.BlockSpec((tm, arr.shape[1]), lambda i: (i, 0))
    full = lambda arr: pl.BlockSpec(arr.shape, lambda i: (0, 0), pipeline_mode=pl.Buffered(1))
    return pl.pallas_call(
        kern,
        grid=(t // tm,),
        in_specs=[row(y), row(g), row(bvg), full(gn_w), full(gn_b), full(ones), row(yb), full(wa),
                  full(wb), row(x), full(lg), full(lb)],
        out_specs=[row(x), row(x)],
        out_shape=[jax.ShapeDtypeStruct((t, d), F32), jax.ShapeDtypeStruct((t, d), BF16)],
        compiler_params=_params("parallel"),
        name="even_outproj_ln",
    )(y, g, bvg, gn_w, gn_b, ones, yb, wa, wb, x, lg, lb)


def _outproj_ln_kernel(*refs, n_in, alpha):
    ys = refs[:n_in]
    ws = refs[n_in:2 * n_in]
    x_ref, g_ref, b_ref, o_ref, obf_ref = refs[2 * n_in:]
    mix = _dot(ys[0][...].astype(BF16), ws[0][...])
    for y, w in zip(ys[1:], ws[1:]):
        mix = mix + _dot(y[...].astype(BF16), w[...])
    out = _layer_norm(alpha * x_ref[...] + mix, g_ref[...], b_ref[...])
    o_ref[...] = out
    obf_ref[...] = out.astype(BF16)


def _outproj_ln(ys, ws, x, g, b, alpha, tm):
    t, d = x.shape
    n_in = len(ys)
    kern = functools.partial(_outproj_ln_kernel, n_in=n_in, alpha=alpha)
    row = lambda width: pl.BlockSpec((tm, width), lambda i: (i, 0))
    full = lambda arr: pl.BlockSpec(arr.shape, lambda i: (0, 0), pipeline_mode=pl.Buffered(1))
    return pl.pallas_call(
        kern,
        grid=(t // tm,),
        in_specs=[row(y.shape[1]) for y in ys] + [full(w) for w in ws] + [row(d), full(g), full(b)],
        out_specs=[row(d), row(d)],
        out_shape=[jax.ShapeDtypeStruct((t, d), F32), jax.ShapeDtypeStruct((t, d), BF16)],
        compiler_params=_params("parallel"),
        name="outproj_ln",
    )(*ys, *ws, x, g, b)


def _ffn_up_kernel(x_ref, w1_ref, w3_ref, o_ref):
    x = x_ref[...]
    h1 = _dot(x, w1_ref[...])
    h3 = _dot(x, w3_ref[...])
    o_ref[...] = (h1 * jax.nn.sigmoid(h1) * h3).astype(o_ref.dtype)


def _ffn_up(x, w1, w3, tm, tf):
    t, d = x.shape
    f = w1.shape[1]
    return pl.pallas_call(
        _ffn_up_kernel,
        grid=(t // tm, f // tf),
        in_specs=[pl.BlockSpec((tm, d), lambda i, j: (i, 0)),
                  pl.BlockSpec((d, tf), lambda i, j: (0, j)),
                  pl.BlockSpec((d, tf), lambda i, j: (0, j))],
        out_specs=pl.BlockSpec((tm, tf), lambda i, j: (i, j)),
        out_shape=jax.ShapeDtypeStruct((t, f), BF16),
        compiler_params=_params("parallel", "arbitrary"),
        name="ffn_up",
    )(x, w1, w3)


def _ffn_down_ple_kernel(a_ref, w2_ref, x_ref, g_ref, b_ref, p_ref, wg_ref, wp_ref, o_ref, obf_ref,
                         *, alpha):
    y = _dot(a_ref[...], w2_ref[...])
    x2 = _layer_norm(alpha * x_ref[...] + y, g_ref[...], b_ref[...])
    gate = jax.nn.sigmoid(_dot(x2.astype(BF16), wg_ref[...]))
    proj = _dot(p_ref[...].astype(BF16), wp_ref[...])
    out = x2 + gate * proj
    o_ref[...] = out
    obf_ref[...] = out.astype(BF16)


def _ffn_down_ple(act, w2, x, g, b, p, layer, wg, wp, alpha, tm):
    t, d = x.shape
    f = act.shape[1]
    pd = p.shape[1]
    p_first = layer * (t // tm)
    kern = functools.partial(_ffn_down_ple_kernel, alpha=alpha)
    row = lambda width: pl.BlockSpec((tm, width), lambda i: (i, 0))
    once = lambda arr: pl.BlockSpec(arr.shape, lambda i: (0, 0), pipeline_mode=pl.Buffered(1))
    return pl.pallas_call(
        kern,
        grid=(t // tm,),
        in_specs=[row(f), once(w2), row(d), once(g), once(b),
                  pl.BlockSpec((tm, pd), lambda i: (p_first + i, 0)), once(wg), once(wp)],
        out_specs=[row(d), row(d)],
        out_shape=[jax.ShapeDtypeStruct((t, d), F32), jax.ShapeDtypeStruct((t, d), BF16)],
        compiler_params=_params("parallel"),
        name="ffn_down_ple",
    )(act, w2, x, g, b, p, wg, wp)


def _even_mixer(xbf, positions, w_in, mu, w0, w_up, a0, a_up, g_up, k_k, k_a, r_k, gn_w, gn_b,
                sinks, batch, seq):
    aw = w0.shape[0]
    dl = w_up.shape[0]
    al = a_up.shape[0]
    gl = g_up.shape[0]
    a_cols = 3 * aw + dl + al + gl
    gl_pad = 384
    assert dl == 64 and al == 64 and gl <= gl_pad
    d = w_in.shape[0]
    pad_cols = jnp.zeros((d, gl_pad - gl), F32)
    w_a = jnp.concatenate([w_in[:, :a_cols], pad_cols], axis=1).astype(BF16)
    w_b = w_in[:, a_cols:].astype(BF16)
    mu_p = jnp.concatenate([mu, jnp.zeros((gl_pad - gl,), F32)])[None, :]
    zeros = jnp.zeros((dl, aw), F32)
    wwa = jnp.concatenate([jnp.concatenate([w_up, zeros], 1), jnp.concatenate([zeros, a_up], 1)], 0)
    gup = jnp.concatenate([g_up, jnp.zeros((gl_pad - gl, aw), F32)], 0)

    h_a = _matmul(xbf, w_a, F32, 2048, 512, "proj_rwkv")
    h_b = _matmul(xbf, w_b, F32, 1024, w_b.shape[1], "proj_swa")

    ones = jnp.kron(jnp.eye(2, dtype=F32), jnp.ones((HEAD_DIM, HEAD_DIM), F32)).astype(BF16)
    at, bt, kt, rt, bh, kh, v, g, bvg, wc = _rwkv_prep(
        h_a, mu_p, w0[None], a0[None], k_k[None], k_a[None], r_k.reshape(1, aw), wwa, gup, ones,
        seq, 256)
    y = _rwkv_scan(at, bt, kt, rt, bh, kh, v, wc, batch, seq)

    qw = sinks.shape[0] * HEAD_DIM
    kvw = (w_b.shape[1] - qw) // 2
    tabs, half = _rope_tables(positions, HEAD_DIM, 128)
    q, kb, vb = _swa_prep(h_b, tabs, half, qw, kvw, 512)
    yb = _swa(q, kb, vb, sinks, batch, seq)
    return (y, g, bvg, gn_w[None], gn_b[None], ones), yb


def _odd_mixer(xbf, positions, w_in, lam_p, subln_g, layer_idx, batch, seq):
    width = w_in.shape[1] // 3
    q_scale = DIFF_HEAD_DIM ** -0.5 * math.log2(math.e)
    w = jnp.concatenate([w_in[:, :width] * q_scale, w_in[:, width:]], axis=1).astype(BF16)
    h = _matmul(xbf, w, BF16, 2048, 1024, "proj_diff")
    tabs, half = _rope_tables(positions, DIFF_HEAD_DIM, 128)
    k_rot = _diff_rope_k(h, tabs, half, width, 512, 1024)
    lam_init = 0.8 - 0.6 * math.exp(-0.3 * layer_idx)
    lv = lam_p.astype(F32)
    lam = jnp.exp(jnp.sum(lv[0] * lv[1])) - jnp.exp(jnp.sum(lv[2] * lv[3])) + lam_init
    scalars = jnp.stack([lam, jnp.asarray(1.0 - lam_init, F32)]).astype(F32)
    return _diff_attn(h, k_rot, tabs, half, scalars, subln_g[None], batch, seq, 1024, 256)


def kernel(x, p, positions, e_w_in, e_mu, e_w0, e_w_up, e_a0, e_a_up, e_g_up, e_k_k, e_k_a, e_r_k,
           e_gn_w, e_gn_b, e_sinks, e_w_out, o_w_in, o_lambda, o_subln_g, o_w_out,
           ln1_g, ln1_b, ln2_g, ln2_b, ffn_w1, ffn_w3, ffn_w2, ple_w_proj, ple_w_gate):
    batch, seq, d = x.shape
    depth = p.shape[0]
    t = batch * seq
    alpha = (2.0 * depth) ** 0.25
    xf = x.reshape(t, d)
    xbf = xf.astype(BF16)
    p_rows = p.reshape(depth * t, p.shape[-1])
    for i in range(depth):
        j = i // 2
        if i % 2 == 0:
            ya, yb = _even_mixer(xbf, positions, e_w_in[j], e_mu[j], e_w0[j], e_w_up[j], e_a0[j],
                                 e_a_up[j], e_g_up[j], e_k_k[j], e_k_a[j], e_r_k[j], e_gn_w[j],
                                 e_gn_b[j], e_sinks[j], batch, seq)
            aw = ya[0].shape[1]
            w_out = e_w_out[j].astype(BF16)
            xf, xbf = _even_outproj_ln(*ya, yb, w_out[:aw], w_out[aw:], xf, ln1_g[i][None],
                                       ln1_b[i][None], alpha, 512)
        else:
            y = _odd_mixer(xbf, positions, o_w_in[j], o_lambda[j], o_subln_g[j], i, batch, seq)
            xf, xbf = _outproj_ln([y], [o_w_out[j].astype(BF16)], xf, ln1_g[i][None],
                                  ln1_b[i][None], alpha, 512)
        act = _ffn_up(xbf, ffn_w1[i].astype(BF16), ffn_w3[i].astype(BF16), 1024, 512)
        xf, xbf = _ffn_down_ple(act, ffn_w2[i].astype(BF16), xf, ln2_g[i][None], ln2_b[i][None],
                                p_rows, i, ple_w_gate[i].astype(BF16),
                                ple_w_proj[i].astype(BF16), alpha, 256)
    return xf.reshape(batch, seq, d)
```

```python
import functools
import math

import jax
import jax.numpy as jnp
from jax import lax
from jax.experimental import pallas as pl
from jax.experimental.pallas import tpu as pltpu

F32 = jnp.float32
BF16 = jnp.bfloat16
HI = lax.Precision.HIGHEST

HEAD_DIM = 64
DIFF_HEAD_DIM = 128
WINDOW = 128
ROPE_THETA = 500000.0
ROPE_FRACTION = 4
LN_EPS = 1e-5
WKV_GN_EPS = 64e-5
SUBLN_EPS = 1e-5
RWKV_CHUNK = 64
RWKV_CHUNKS_PER_STEP = 4
VMEM_LIMIT = 56 * 1024 * 1024


def _dot(a, b, prec=None):
    return jnp.dot(a, b, preferred_element_type=F32, precision=prec)


def _dot_nt(a, b, prec=None):
    return lax.dot_general(a, b, (((1,), (1,)), ((), ())), preferred_element_type=F32, precision=prec)


def _dot_tn(a, b, prec=None):
    return lax.dot_general(a, b, (((0,), (0,)), ((), ())), preferred_element_type=F32, precision=prec)


def _params(*sem):
    return pltpu.CompilerParams(dimension_semantics=sem, vmem_limit_bytes=VMEM_LIMIT)


def _layer_norm(z, g, b):
    mu = jnp.mean(z, -1, keepdims=True)
    d = z - mu
    var = jnp.mean(d * d, -1, keepdims=True)
    return d * lax.rsqrt(var + LN_EPS) * g + b


def _mm_kernel(x_ref, w_ref, o_ref):
    o_ref[...] = _dot(x_ref[...], w_ref[...]).astype(o_ref.dtype)


def _matmul(x, w, out_dtype, tm, tn, name):
    t, k = x.shape
    n = w.shape[1]
    return pl.pallas_call(
        _mm_kernel,
        grid=(t // tm, n // tn),
        in_specs=[pl.BlockSpec((tm, k), lambda i, j: (i, 0)),
                  pl.BlockSpec((k, tn), lambda i, j: (0, j))],
        out_specs=pl.BlockSpec((tm, tn), lambda i, j: (i, j)),
        out_shape=jax.ShapeDtypeStruct((t, n), out_dtype),
        compiler_params=_params("parallel", "arbitrary"),
        name=name,
    )(x, w)


def _bf16_parts(x, n):
    parts = []
    for _ in range(n):
        p = x.astype(BF16)
        parts.append(p)
        x = x - p.astype(F32)
    return parts


def _split_dot(x, w, n=2):
    return sum(_dot(p, w) for p in _bf16_parts(x, n))


def _dot3(a, b):
    ah, al = _bf16_parts(a, 2)
    bh, bl = _bf16_parts(b, 2)
    return _dot(ah, bh) + _dot(ah, bl) + _dot(al, bh)


def _head_sums(x, ones):
    gw = ones.shape[0]
    parts = [_split_dot(x[:, i:i + gw], ones) for i in range(0, x.shape[1], gw)]
    return jnp.concatenate(parts, axis=1)


def _rwkv_prep_kernel(h_ref, hp_ref, mu_ref, w0_ref, a0_ref, kk_ref, ka_ref, rk_ref, wwa_ref,
                      gup_ref, ones_ref, at_out, bt_out, kt_out, rt_out, bh_out, kh_out, v_out,
                      g_out, bvg_out, wc_out, *, blocks_per_seq, aw, chunk):
    i = pl.program_id(0)
    h = h_ref[...]
    tm = h.shape[0]
    first = (i % blocks_per_seq) == 0
    prev_row = jnp.where(first, 0.0, hp_ref[7:8, :])
    shifted = pltpu.roll(h, 1, axis=0)
    row = lax.broadcasted_iota(jnp.int32, h.shape, 0)
    shifted = jnp.where(row == 0, prev_row, shifted)
    hm = h + (shifted - h) * mu_ref[...]
    r = hm[:, 0:aw]
    k = hm[:, aw:2 * aw]
    v = hm[:, 2 * aw:3 * aw]
    z = hm[:, 3 * aw:3 * aw + 128]
    lane = lax.broadcasted_iota(jnp.int32, z.shape, 1)
    z = jnp.where(lane < 64, jnp.tanh(z), z)
    wa = _dot3(z, wwa_ref[...])
    w = -jax.nn.softplus(-(w0_ref[...] + wa[:, :aw])) - 0.5
    a = jax.nn.sigmoid(a0_ref[...] + wa[:, aw:])
    lg = hm[:, 3 * aw + 128:]
    g = _dot3(jax.nn.sigmoid(lg), gup_ref[...])
    lw = -jnp.exp(w)

    ri = lax.broadcasted_iota(jnp.int32, (tm, tm), 0)
    ci = lax.broadcasted_iota(jnp.int32, (tm, tm), 1)
    shift = int(math.log2(chunk))
    same = (ri >> shift) == (ci >> shift)
    lw_parts = _bf16_parts(lw, 3)
    tri = jnp.where(same & (ri >= ci), 1.0, 0.0).astype(BF16)
    blk = jnp.where(same, 1.0, 0.0).astype(BF16)
    cs = sum(_dot(tri, part) for part in lw_parts)
    tot = sum(_dot(blk, part) for part in lw_parts)

    ones = ones_ref[...]
    kkr = k * kk_ref[...]
    kk = kkr / jnp.maximum(jnp.sqrt(_head_sums(kkr * kkr, ones)), 1e-12)
    k2 = k * (1.0 + (a - 1.0) * ka_ref[...])
    bb = kk * a
    w_inv = jnp.exp(-cs)
    w_rem = jnp.exp(tot - cs)
    at_out[...] = (-kk * jnp.exp(cs - lw)).astype(BF16)
    bt_out[...] = (bb * w_inv).astype(BF16)
    kt_out[...] = (k2 * w_inv).astype(BF16)
    rt_out[...] = (r * jnp.exp(cs)).astype(BF16)
    bh_out[...] = (bb * w_rem).astype(BF16)
    kh_out[...] = (k2 * w_rem).astype(BF16)
    v_out[...] = v.astype(BF16)
    g_out[...] = g
    bvg_out[...] = _head_sums(r * k2 * rk_ref[...], ones) * v * g
    for c in range(tm // chunk):
        wc_out[c] = jnp.exp(tot[c * chunk:c * chunk + 1, :])


def _rwkv_prep(h, mu, w0, a0, k_k, k_a, r_k, wwa, gup, ones, seq, tm):
    t, cols = h.shape
    aw = w0.shape[1]
    chunk = RWKV_CHUNK
    row_spec = pl.BlockSpec((tm, aw), lambda i: (i, 0))
    full = lambda arr: pl.BlockSpec(arr.shape, lambda i: (0, 0))
    kern = functools.partial(_rwkv_prep_kernel, blocks_per_seq=seq // tm, aw=aw, chunk=chunk)
    return pl.pallas_call(
        kern,
        grid=(t // tm,),
        in_specs=[pl.BlockSpec((tm, cols), lambda i: (i, 0)),
                  pl.BlockSpec((8, cols), lambda i: (jnp.maximum(i * (tm // 8) - 1, 0), 0)),
                  full(mu), full(w0), full(a0), full(k_k), full(k_a), full(r_k), full(wwa),
                  full(gup), full(ones)],
        out_specs=[row_spec] * 9 + [pl.BlockSpec((tm // chunk, 1, aw), lambda i: (i, 0, 0))],
        out_shape=([jax.ShapeDtypeStruct((t, aw), BF16)] * 7 + [jax.ShapeDtypeStruct((t, aw), F32)] * 2
                   + [jax.ShapeDtypeStruct((t // chunk, 1, aw), F32)]),
        compiler_params=_params("parallel"),
        name="rwkv_prep",
    )(h, h, mu, w0, a0, k_k, k_a, r_k, wwa, gup, ones)


def _rwkv_scan_kernel(at_ref, bt_ref, kt_ref, rt_ref, bh_ref, kh_ref, v_ref, wc_ref, y_ref, st_ref,
                      *, pairs, n):
    c = n
    n2 = 2 * n
    nsub = at_ref.shape[0] // c

    @pl.when(pl.program_id(1) == 0)
    def _():
        st_ref[...] = jnp.zeros_like(st_ref)

    row = lax.broadcasted_iota(jnp.int32, (n2, n2), 0)
    col = lax.broadcasted_iota(jnp.int32, (n2, n2), 1)
    same = (row >= c) == (col >= c)
    strict = same & (row > col)
    incl = same & (row >= col)
    incl2 = jnp.concatenate([incl, incl], axis=1)
    eye = (row == col).astype(F32)
    lo = lax.broadcasted_iota(jnp.int32, (c, n2), 1) < n

    def stack(ref, q, sl):
        x = ref[q * c:(q + 1) * c, sl]
        zero = jnp.zeros_like(x)
        return jnp.concatenate([jnp.where(lo, x, zero), jnp.where(lo, zero, x)], axis=0)

    sls = [slice(p * n2, (p + 1) * n2) for p in range(pairs)]
    units = [(q, p) for q in range(nsub) for p in range(pairs)]
    stacks = {u: [stack(ref, u[0], sls[u[1]])
                  for ref in (at_ref, bt_ref, kt_ref, rt_ref, bh_ref, kh_ref, v_ref)] for u in units}
    ar = {u: jnp.concatenate([stacks[u][0], stacks[u][3]], axis=0) for u in units}
    big = {u: _dot_nt(ar[u], jnp.concatenate([stacks[u][1], stacks[u][2]], axis=0)) for u in units}
    l_ab = {u: jnp.where(strict, big[u][:n2, :n2], 0.0) for u in units}
    l_ak = {u: jnp.where(strict, big[u][:n2, n2:], 0.0).astype(BF16) for u in units}
    m_all = {u: jnp.where(incl2, big[u][n2:, :], 0.0).astype(BF16) for u in units}

    tinv = {u: eye + l_ab[u] for u in units}
    pw = {u: l_ab[u].astype(BF16) for u in units}
    for _ in range(int(math.log2(c)) - 1):
        pw = {u: _dot(pw[u], pw[u]).astype(BF16) for u in units}
        tinv = {u: tinv[u] + _dot(tinv[u].astype(BF16), pw[u]) for u in units}
    tinv = {u: tinv[u].astype(BF16) for u in units}
    lv = {u: _dot(l_ak[u], stacks[u][6]) for u in units}

    state = [st_ref[p] for p in range(pairs)]
    for q in range(nsub):
        us = [(q, p) for p in range(pairs)]
        from_state = [_dot_nt(ar[u], state[u[1]].astype(BF16)) for u in us]
        x1 = [(from_state[p][:n2] + lv[us[p]]).astype(BF16) for p in range(pairs)]
        uu = [_dot(tinv[us[p]], x1[p]).astype(BF16) for p in range(pairs)]
        uv = [jnp.concatenate([uu[p], stacks[us[p]][6]], axis=0) for p in range(pairs)]
        for p in range(pairs):
            u = us[p]
            yst = from_state[p][n2:] + _dot(m_all[u], uv[p])
            state[p] = state[p] * wc_ref[q, :, sls[p]] + _dot_tn(
                uv[p], jnp.concatenate([stacks[u][4], stacks[u][5]], axis=0))
            y_ref[q * c:(q + 1) * c, sls[p]] = yst[:c] + yst[c:]
    for p in range(pairs):
        st_ref[p] = state[p]


def _rwkv_scan(at, bt, kt, rt, bh, kh, v, wc, batch, seq):
    t, aw = at.shape
    c = RWKV_CHUNK
    n = HEAD_DIM
    assert c == n
    pairs = aw // (2 * n)
    nsub = RWKV_CHUNKS_PER_STEP
    nstep = seq // (c * nsub)
    tok = pl.BlockSpec((c * nsub, aw), lambda b, j: (b * nstep + j, 0))
    kern = functools.partial(_rwkv_scan_kernel, pairs=pairs, n=n)
    return pl.pallas_call(
        kern,
        grid=(batch, nstep),
        in_specs=[tok] * 7 + [pl.BlockSpec((nsub, 1, aw), lambda b, j: (b * nstep + j, 0, 0))],
        out_specs=tok,
        out_shape=jax.ShapeDtypeStruct((t, aw), F32),
        scratch_shapes=[pltpu.VMEM((pairs, 2 * n, 2 * n), F32)],
        compiler_params=_params("parallel", "arbitrary"),
        name="rwkv_scan",
    )(at, bt, kt, rt, bh, kh, v, wc)


def _rope_tables(positions, head_dim, lanes):
    rd = head_dim // ROPE_FRACTION
    half = rd // 2
    inv_freq = ROPE_THETA ** (-jnp.arange(half, dtype=F32) / half)
    ang = positions.astype(F32).reshape(-1, 1) * inv_freq
    cos, sin = jnp.cos(ang), jnp.sin(ang)
    t = ang.shape[0]
    pad = jnp.zeros((t, head_dim - rd), F32)
    zero = jnp.zeros((t, half), F32)
    c = jnp.concatenate([cos, cos, pad + 1.0], -1)
    sn = jnp.concatenate([-sin, zero, pad], -1)
    sp = jnp.concatenate([zero, sin, pad], -1)
    rep = lanes // head_dim
    return tuple(jnp.tile(x, (1, rep)) for x in (c, sn, sp)), half


def _rope(x, c, sn, sp, half):
    width = x.shape[1]
    rep = width // c.shape[1]
    if rep > 1:
        c, sn, sp = (jnp.concatenate([y] * rep, axis=1) for y in (c, sn, sp))
    nxt = pltpu.roll(x, width - half, axis=1)
    prv = pltpu.roll(x, half, axis=1)
    return x * c + nxt * sn + prv * sp


def _swa_kernel(sink_ref, q_ref, kvp_ref, kvc_ref, c_ref, sn_ref, sp_ref, cp_ref, snp_ref, spp_ref,
                o_ref, *, q_heads, n, half, scale):
    blk = q_ref.shape[0]
    j = pl.program_id(1)
    n2 = 2 * n
    pairs = q_heads // 2
    tabs = (c_ref[...], sn_ref[...], sp_ref[...])
    q = (_rope(q_ref[...], *tabs, half) * scale).astype(BF16)
    kvp, kvc = kvp_ref[...], kvc_ref[...]
    k2 = jnp.concatenate([_rope(kvp[:, :n2], cp_ref[...], snp_ref[...], spp_ref[...], half),
                          _rope(kvc[:, :n2], *tabs, half)], axis=0)
    v2 = jnp.concatenate([kvp[:, n2:], kvc[:, n2:]], axis=0)
    lo2 = lax.broadcasted_iota(jnp.int32, (2 * blk, n2), 1) < n
    dup = lambda x: [jnp.where(lo2, x, pltpu.roll(x, n, axis=1)).astype(BF16),
                     jnp.where(lo2, pltpu.roll(x, n, axis=1), x).astype(BF16)]
    kd, vd = dup(k2), dup(v2)

    qi = lax.broadcasted_iota(jnp.int32, (2 * blk, 2 * blk), 0) % blk
    kj = lax.broadcasted_iota(jnp.int32, (2 * blk, 2 * blk), 1)
    rel = qi + blk - kj
    mask = (rel >= 0) & (rel < WINDOW) & ((kj >= blk) | (j > 0))
    upper = lax.broadcasted_iota(jnp.int32, (2 * blk, 1), 0) < blk
    lo = lax.broadcasted_iota(jnp.int32, (blk, n2), 1) < n
    zero = jnp.zeros((blk, n2), BF16)
    per_group = pairs // (k2.shape[1] // n)

    ss = []
    for p in range(pairs):
        qp = q[:, p * n2:(p + 1) * n2]
        lhs = jnp.concatenate([jnp.where(lo, qp, zero), jnp.where(lo, zero, qp)], axis=0)
        ss.append(_dot_nt(lhs, kd[p // per_group]))
    ps = []
    for p in range(pairs):
        s = jnp.where(mask, ss[p], -jnp.inf)
        sink = jnp.where(upper, sink_ref[2 * p], sink_ref[2 * p + 1])
        m = jnp.maximum(jnp.max(s, -1, keepdims=True), sink)
        e = jnp.exp(s - m)
        den = jnp.sum(e, -1, keepdims=True) + jnp.exp(sink - m)
        ps.append((e / den).astype(BF16))
    for p in range(pairs):
        o2 = _dot(ps[p], vd[p // per_group])
        o_ref[:, p * n2:(p + 1) * n2] = jnp.where(lo, o2[:blk], o2[blk:]).astype(o_ref.dtype)


def _swa(hb, tabs, half, sinks, batch, seq):
    t = hb.shape[0]
    n = HEAD_DIM
    qw = sinks.shape[0] * n
    kvw2 = hb.shape[1] - qw
    blk = WINDOW
    nb = seq // blk
    kv_col = qw // kvw2
    cur = lambda b, j: (b * nb + j, 0)
    prev = lambda b, j: (b * nb + jnp.maximum(j - 1, 0), 0)
    cur_kv = lambda b, j: (b * nb + j, kv_col)
    prev_kv = lambda b, j: (b * nb + jnp.maximum(j - 1, 0), kv_col)
    kern = functools.partial(_swa_kernel, q_heads=qw // n, n=n, half=half, scale=n ** -0.5)
    tab = lambda idx: pl.BlockSpec((blk, 128), idx)
    return pl.pallas_call(
        kern,
        grid=(batch, nb),
        in_specs=[pl.BlockSpec(memory_space=pltpu.SMEM),
                  pl.BlockSpec((blk, qw), cur),
                  pl.BlockSpec((blk, kvw2), prev_kv), pl.BlockSpec((blk, kvw2), cur_kv),
                  tab(cur), tab(cur), tab(cur), tab(prev), tab(prev), tab(prev)],
        out_specs=pl.BlockSpec((blk, qw), cur),
        out_shape=jax.ShapeDtypeStruct((t, qw), F32),
        compiler_params=_params("parallel", "arbitrary"),
        name="swa",
    )(sinks, hb, hb, hb, *tabs, *tabs)


def _diff_rope_kernel(x_ref, c_ref, sn_ref, sp_ref, o_ref, *, half):
    x = x_ref[...].astype(F32)
    o_ref[...] = _rope(x, c_ref[...], sn_ref[...], sp_ref[...], half).astype(o_ref.dtype)


def _diff_rope_k(h, tabs, half, width, tm, tn):
    t = h.shape[0]
    first = width // tn
    tab_spec = pl.BlockSpec((tm, 128), lambda i, j: (i, 0))
    return pl.pallas_call(
        functools.partial(_diff_rope_kernel, half=half),
        grid=(t // tm, width // tn),
        in_specs=[pl.BlockSpec((tm, tn), lambda i, j: (i, first + j)), tab_spec, tab_spec, tab_spec],
        out_specs=pl.BlockSpec((tm, tn), lambda i, j: (i, j)),
        out_shape=jax.ShapeDtypeStruct((t, width), h.dtype),
        compiler_params=_params("parallel", "arbitrary"),
        name="diff_rope_k",
    )(h, *tabs)


def _diff_attn_kernel(sc_ref, q_ref, c_ref, sn_ref, sp_ref, k_ref, v_ref, g_ref, o_ref, m_ref, l_ref,
                      acc_ref, qs_ref, *, blk, rg, dh, half):
    qi = pl.program_id(2)
    qs_ref[...] = _rope(q_ref[...].astype(F32), c_ref[...], sn_ref[...], sp_ref[...],
                        half).astype(qs_ref.dtype)
    groups = blk // rg
    vrep = 2 * dh // 128

    def block(j, diagonal, first):
        off = pl.multiple_of(j * blk, blk)
        nkeys = [(r + 1) * rg if diagonal else blk for r in range(groups)]
        rows = [slice(r * rg, (r + 1) * rg) for r in range(groups)]
        ss = [[_dot_nt(qs_ref[rows[r], c * dh:(c + 1) * dh],
                       k_ref[pl.ds(off, nkeys[r]), c * dh:(c + 1) * dh])
               for c in range(2)] for r in range(groups)]
        ps, alphas = [], []
        for r, rs in enumerate(rows):
            if diagonal:
                row = lax.broadcasted_iota(jnp.int32, (rg, nkeys[r]), 0) + r * rg
                col = lax.broadcasted_iota(jnp.int32, (rg, nkeys[r]), 1)
                ss[r] = [jnp.where(col <= row, s, -jnp.inf) for s in ss[r]]
            pr, ar = [], []
            for c in range(2):
                top = jnp.max(ss[r][c], -1, keepdims=True)
                if first:
                    m_new = jnp.broadcast_to(top, (rg, 128))
                else:
                    m_old = m_ref[c, rs, :]
                    m_new = jnp.maximum(m_old, top)
                    alpha = jnp.exp2(m_old - m_new)
                p = jnp.exp2(ss[r][c] - jnp.concatenate([m_new] * (nkeys[r] // 128), axis=1))
                psum = jnp.sum(p, -1, keepdims=True)
                if first:
                    l_ref[c, rs, :] = jnp.broadcast_to(psum, (rg, 128))
                else:
                    l_ref[c, rs, :] = alpha * l_ref[c, rs, :] + psum
                    ar.append(jnp.concatenate([alpha] * vrep, axis=1))
                m_ref[c, rs, :] = m_new
                pr.append(p.astype(v_ref.dtype))
            ps.append(pr)
            alphas.append(ar)
        v_all = v_ref[pl.ds(off, blk), :]
        for r, rs in enumerate(rows):
            v = v_all[:nkeys[r]]
            for c in range(2):
                pv = _dot(ps[r][c], v)
                acc_ref[c, rs, :] = pv if first else alphas[r][c] * acc_ref[c, rs, :] + pv

    @pl.when(qi == 0)
    def _():
        block(0, True, True)

    @pl.when(qi > 0)
    def _():
        block(0, False, True)

        def body(j, carry):
            block(j, False, False)
            return carry

        lax.fori_loop(1, qi, body, 0)
        block(qi, True, False)

    lam = sc_ref[0]
    post = sc_ref[1]
    o1 = acc_ref[0] / jnp.concatenate([l_ref[0]] * vrep, axis=1)
    o2 = acc_ref[1] / jnp.concatenate([l_ref[1]] * vrep, axis=1)
    out = o1 - lam * o2
    out = out * lax.rsqrt(jnp.mean(out * out, -1, keepdims=True) + SUBLN_EPS)
    o_ref[...] = (out * g_ref[...] * post).astype(o_ref.dtype)


def _diff_attn(h, k_rot, tabs, half, scalars, subln_g, batch, seq, blk, rg):
    t, width = k_rot.shape
    dh = DIFF_HEAD_DIM
    hw = 2 * dh
    heads = width // hw
    nq = seq // blk
    kern = functools.partial(_diff_attn_kernel, blk=blk, rg=rg, dh=dh, half=half)
    resident = lambda first, bufs: pl.BlockSpec((seq, hw), lambda b, h, qi: (b, first + h),
                                                pipeline_mode=pl.Buffered(bufs))
    tab_spec = pl.BlockSpec((blk, 128), lambda b, h, qi: (b * nq + qi, 0))
    return pl.pallas_call(
        kern,
        grid=(batch, heads, nq),
        in_specs=[pl.BlockSpec(memory_space=pltpu.SMEM),
                  pl.BlockSpec((blk, hw), lambda b, h, qi: (b * nq + qi, h)),
                  tab_spec, tab_spec, tab_spec,
                  resident(0, 2), resident(2 * heads, 1),
                  pl.BlockSpec((1, hw), lambda b, h, qi: (0, 0))],
        out_specs=pl.BlockSpec((blk, hw), lambda b, h, qi: (b * nq + qi, h)),
        out_shape=jax.ShapeDtypeStruct((t, width), BF16),
        scratch_shapes=[pltpu.VMEM((2, blk, 128), F32), pltpu.VMEM((2, blk, 128), F32),
                        pltpu.VMEM((2, blk, hw), F32), pltpu.VMEM((blk, hw), BF16)],
        compiler_params=_params("parallel", "parallel", "arbitrary"),
        name="diff_attn",
    )(scalars, h, *tabs, k_rot, h, subln_g)


def _rwkv_finish(y, g, bvg, gnw, gnb, ones, n):
    ym = _head_sums(y, ones) * (1.0 / n)
    d = y - ym
    yv = _head_sums(d * d, ones) * (1.0 / n)
    return (d * lax.rsqrt(yv + WKV_GN_EPS) * gnw + gnb) * g + bvg


def _even_outproj_ln_kernel(y_ref, g_ref, bvg_ref, gnw_ref, gnb_ref, ones_ref, yb_ref, wa_ref, wb_ref,
                            x_ref, lg_ref, lb_ref, o_ref, obf_ref, *, alpha, n):
    ya = _rwkv_finish(y_ref[...], g_ref[...], bvg_ref[...], gnw_ref[...], gnb_ref[...],
                      ones_ref[...], n)
    mix = _dot(ya.astype(BF16), wa_ref[...]) + _dot(yb_ref[...].astype(BF16), wb_ref[...])
    out = _layer_norm(alpha * x_ref[...] + mix, lg_ref[...], lb_ref[...])
    o_ref[...] = out
    obf_ref[...] = out.astype(BF16)


def _even_outproj_ln(y, g, bvg, gn_w, gn_b, ones, yb, wa, wb, x, lg, lb, alpha, tm):
    t, d = x.shape
    kern = functools.partial(_even_outproj_ln_kernel, alpha=alpha, n=HEAD_DIM)
    row = lambda arr: pl.BlockSpec((tm, arr.shape[1]), lambda i: (i, 0))
    full = lambda arr: pl.BlockSpec(arr.shape, lambda i: (0, 0), pipeline_mode=pl.Buffered(1))
    return pl.pallas_call(
        kern,
        grid=(t // tm,),
        in_specs=[row(y), row(g), row(bvg), full(gn_w), full(gn_b), full(ones), row(yb), full(wa),
                  full(wb), row(x), full(lg), full(lb)],
        out_specs=[row(x), row(x)],
        out_shape=[jax.ShapeDtypeStruct((t, d), F32), jax.ShapeDtypeStruct((t, d), BF16)],
        compiler_params=_params("parallel"),
        name="even_outproj_ln",
    )(y, g, bvg, gn_w, gn_b, ones, yb, wa, wb, x, lg, lb)


def _outproj_ln_kernel(*refs, n_in, alpha):
    ys = refs[:n_in]
    ws = refs[n_in:2 * n_in]
    x_ref, g_ref, b_ref, o_ref, obf_ref = refs[2 * n_in:]
    mix = _dot(ys[0][...].astype(BF16), ws[0][...])
    for y, w in zip(ys[1:], ws[1:]):
        mix = mix + _dot(y[...].astype(BF16), w[...])
    out = _layer_norm(alpha * x_ref[...] + mix, g_ref[...], b_ref[...])
    o_ref[...] = out
    obf_ref[...] = out.astype(BF16)


def _outproj_ln(ys, ws, x, g, b, alpha, tm):
    t, d = x.shape
    n_in = len(ys)
    kern = functools.partial(_outproj_ln_kernel, n_in=n_in, alpha=alpha)
    row = lambda width: pl.BlockSpec((tm, width), lambda i: (i, 0))
    full = lambda arr: pl.BlockSpec(arr.shape, lambda i: (0, 0), pipeline_mode=pl.Buffered(1))
    return pl.pallas_call(
        kern,
        grid=(t // tm,),
        in_specs=[row(y.shape[1]) for y in ys] + [full(w) for w in ws] + [row(d), full(g), full(b)],
        out_specs=[row(d), row(d)],
        out_shape=[jax.ShapeDtypeStruct((t, d), F32), jax.ShapeDtypeStruct((t, d), BF16)],
        compiler_params=_params("parallel"),
        name="outproj_ln",
    )(*ys, *ws, x, g, b)


def _ffn_up_kernel(x_ref, w1_ref, w3_ref, o_ref):
    x = x_ref[...]
    h1 = _dot(x, w1_ref[...].astype(BF16))
    h3 = _dot(x, w3_ref[...].astype(BF16))
    o_ref[...] = (h1 * jax.nn.sigmoid(h1) * h3).astype(o_ref.dtype)


def _ffn_up(x, w1, w3, layer, tm, tf):
    t, d = x.shape
    f = w1.shape[2]
    return pl.pallas_call(
        _ffn_up_kernel,
        grid=(t // tm, f // tf),
        in_specs=[pl.BlockSpec((tm, d), lambda i, j: (i, 0)),
                  pl.BlockSpec((None, d, tf), lambda i, j: (layer, 0, j)),
                  pl.BlockSpec((None, d, tf), lambda i, j: (layer, 0, j))],
        out_specs=pl.BlockSpec((tm, tf), lambda i, j: (i, j)),
        out_shape=jax.ShapeDtypeStruct((t, f), BF16),
        compiler_params=_params("parallel", "arbitrary"),
        name="ffn_up",
    )(x, w1, w3)


def _ffn_down_ple_kernel(a_ref, w2_ref, x_ref, g_ref, b_ref, p_ref, wg_ref, wp_ref, o_ref, obf_ref,
                         *, alpha):
    y = _dot(a_ref[...], w2_ref[...])
    x2 = _layer_norm(alpha * x_ref[...] + y, g_ref[...], b_ref[...])
    gate = jax.nn.sigmoid(_dot(x2.astype(BF16), wg_ref[...]))
    proj = _dot(p_ref[...].astype(BF16), wp_ref[...])
    out = x2 + gate * proj
    o_ref[...] = out
    obf_ref[...] = out.astype(BF16)


def _ffn_down_ple(act, w2, x, g, b, p, layer, wg, wp, alpha, tm):
    t, d = x.shape
    f = act.shape[1]
    pd = p.shape[1]
    p_first = layer * (t // tm)
    kern = functools.partial(_ffn_down_ple_kernel, alpha=alpha)
    row = lambda width: pl.BlockSpec((tm, width), lambda i: (i, 0))
    once = lambda arr: pl.BlockSpec(arr.shape, lambda i: (0, 0), pipeline_mode=pl.Buffered(1))
    return pl.pallas_call(
        kern,
        grid=(t // tm,),
        in_specs=[row(f), once(w2), row(d), once(g), once(b),
                  pl.BlockSpec((tm, pd), lambda i: (p_first + i, 0)), once(wg), once(wp)],
        out_specs=[row(d), row(d)],
        out_shape=[jax.ShapeDtypeStruct((t, d), F32), jax.ShapeDtypeStruct((t, d), BF16)],
        compiler_params=_params("parallel"),
        name="ffn_down_ple",
    )(act, w2, x, g, b, p, wg, wp)


def _even_mixer(xbf, positions, w_in, mu, w0, w_up, a0, a_up, g_up, k_k, k_a, r_k, gn_w, gn_b,
                sinks, batch, seq):
    aw = w0.shape[0]
    dl = w_up.shape[0]
    al = a_up.shape[0]
    gl = g_up.shape[0]
    a_cols = 3 * aw + dl + al + gl
    gl_pad = 384
    assert dl == 64 and al == 64 and gl <= gl_pad
    d = w_in.shape[0]
    pad_cols = jnp.zeros((d, gl_pad - gl), F32)
    w_a = jnp.concatenate([w_in[:, :a_cols], pad_cols], axis=1).astype(BF16)
    w_b = w_in[:, a_cols:].astype(BF16)
    mu_p = jnp.concatenate([mu, jnp.zeros((gl_pad - gl,), F32)])[None, :]
    zeros = jnp.zeros((dl, aw), F32)
    wwa = jnp.concatenate([jnp.concatenate([w_up, zeros], 1), jnp.concatenate([zeros, a_up], 1)], 0)
    gup = jnp.concatenate([g_up, jnp.zeros((gl_pad - gl, aw), F32)], 0)

    h_a = _matmul(xbf, w_a, F32, 2048, 512, "proj_rwkv")
    h_b = _matmul(xbf, w_b, F32, 1024, w_b.shape[1], "proj_swa")

    ones = jnp.kron(jnp.eye(2, dtype=F32), jnp.ones((HEAD_DIM, HEAD_DIM), F32)).astype(BF16)
    at, bt, kt, rt, bh, kh, v, g, bvg, wc = _rwkv_prep(
        h_a, mu_p, w0[None], a0[None], k_k[None], k_a[None], r_k.reshape(1, aw), wwa, gup, ones,
        seq, 256)
    y = _rwkv_scan(at, bt, kt, rt, bh, kh, v, wc, batch, seq)

    tabs, half = _rope_tables(positions, HEAD_DIM, 128)
    yb = _swa(h_b, tabs, half, sinks, batch, seq)
    return (y, g, bvg, gn_w[None], gn_b[None], ones), yb


def _odd_mixer(xbf, positions, w_in, lam_p, subln_g, layer_idx, batch, seq):
    width = w_in.shape[1] // 3
    q_scale = DIFF_HEAD_DIM ** -0.5 * math.log2(math.e)
    w = jnp.concatenate([w_in[:, :width] * q_scale, w_in[:, width:]], axis=1).astype(BF16)
    h = _matmul(xbf, w, BF16, 2048, 1024, "proj_diff")
    tabs, half = _rope_tables(positions, DIFF_HEAD_DIM, 128)
    k_rot = _diff_rope_k(h, tabs, half, width, 512, 1024)
    lam_init = 0.8 - 0.6 * math.exp(-0.3 * layer_idx)
    lv = lam_p.astype(F32)
    lam = jnp.exp(jnp.sum(lv[0] * lv[1])) - jnp.exp(jnp.sum(lv[2] * lv[3])) + lam_init
    scalars = jnp.stack([lam, jnp.asarray(1.0 - lam_init, F32)]).astype(F32)
    return _diff_attn(h, k_rot, tabs, half, scalars, subln_g[None], batch, seq, 1024, 256)


def kernel(x, p, positions, e_w_in, e_mu, e_w0, e_w_up, e_a0, e_a_up, e_g_up, e_k_k, e_k_a, e_r_k,
           e_gn_w, e_gn_b, e_sinks, e_w_out, o_w_in, o_lambda, o_subln_g, o_w_out,
           ln1_g, ln1_b, ln2_g, ln2_b, ffn_w1, ffn_w3, ffn_w2, ple_w_proj, ple_w_gate):
    batch, seq, d = x.shape
    depth = p.shape[0]
    t = batch * seq
    alpha = (2.0 * depth) ** 0.25
    xf = x.reshape(t, d)
    xbf = xf.astype(BF16)
    p_rows = p.reshape(depth * t, p.shape[-1])
    for i in range(depth):
        j = i // 2
        if i % 2 == 0:
            ya, yb = _even_mixer(xbf, positions, e_w_in[j], e_mu[j], e_w0[j], e_w_up[j], e_a0[j],
                                 e_a_up[j], e_g_up[j], e_k_k[j], e_k_a[j], e_r_k[j], e_gn_w[j],
                                 e_gn_b[j], e_sinks[j], batch, seq)
            aw = ya[0].shape[1]
            w_out = e_w_out[j].astype(BF16)
            xf, xbf = _even_outproj_ln(*ya, yb, w_out[:aw], w_out[aw:], xf, ln1_g[i][None],
                                       ln1_b[i][None], alpha, 512)
        else:
            y = _odd_mixer(xbf, positions, o_w_in[j], o_lambda[j], o_subln_g[j], i, batch, seq)
            xf, xbf = _outproj_ln([y], [o_w_out[j].astype(BF16)], xf, ln1_g[i][None],
                                  ln1_b[i][None], alpha, 512)
        act = _ffn_up(xbf, ffn_w1, ffn_w3, i, 1024, 512)
        xf, xbf = _ffn_down_ple(act, ffn_w2[i].astype(BF16), xf, ln2_g[i][None], ln2_b[i][None],
                                p_rows, i, ple_w_gate[i].astype(BF16),
                                ple_w_proj[i].astype(BF16), alpha, 256)
    return xf.reshape(batch, seq, d)
```

```python
import functools
import math

import jax
import jax.numpy as jnp
from jax import lax
from jax.experimental import pallas as pl
from jax.experimental.pallas import tpu as pltpu

F32 = jnp.float32
BF16 = jnp.bfloat16
HI = lax.Precision.HIGHEST

HEAD_DIM = 64
DIFF_HEAD_DIM = 128
WINDOW = 128
ROPE_THETA = 500000.0
ROPE_FRACTION = 4
LN_EPS = 1e-5
WKV_GN_EPS = 64e-5
SUBLN_EPS = 1e-5
RWKV_CHUNK = 64
RWKV_CHUNKS_PER_STEP = 4
VMEM_LIMIT = 56 * 1024 * 1024


def _dot(a, b, prec=None):
    return jnp.dot(a, b, preferred_element_type=F32, precision=prec)


def _dot_nt(a, b, prec=None):
    return lax.dot_general(a, b, (((1,), (1,)), ((), ())), preferred_element_type=F32, precision=prec)


def _dot_tn(a, b, prec=None):
    return lax.dot_general(a, b, (((0,), (0,)), ((), ())), preferred_element_type=F32, precision=prec)


def _params(*sem):
    return pltpu.CompilerParams(dimension_semantics=sem, vmem_limit_bytes=VMEM_LIMIT)


def _layer_norm(z, g, b):
    mu = jnp.mean(z, -1, keepdims=True)
    d = z - mu
    var = jnp.mean(d * d, -1, keepdims=True)
    return d * lax.rsqrt(var + LN_EPS) * g + b


def _mm_kernel(x_ref, w_ref, o_ref):
    o_ref[...] = _dot(x_ref[...], w_ref[...]).astype(o_ref.dtype)


def _matmul(x, w, out_dtype, tm, tn, name):
    t, k = x.shape
    n = w.shape[1]
    return pl.pallas_call(
        _mm_kernel,
        grid=(t // tm, n // tn),
        in_specs=[pl.BlockSpec((tm, k), lambda i, j: (i, 0)),
                  pl.BlockSpec((k, tn), lambda i, j: (0, j))],
        out_specs=pl.BlockSpec((tm, tn), lambda i, j: (i, j)),
        out_shape=jax.ShapeDtypeStruct((t, n), out_dtype),
        compiler_params=_params("parallel", "arbitrary"),
        name=name,
    )(x, w)


def _bf16_parts(x, n):
    parts = []
    for _ in range(n):
        p = x.astype(BF16)
        parts.append(p)
        x = x - p.astype(F32)
    return parts


def _split_dot(x, w, n=2):
    return sum(_dot(p, w) for p in _bf16_parts(x, n))


def _dot3(a, b):
    ah, al = _bf16_parts(a, 2)
    bh, bl = _bf16_parts(b, 2)
    return _dot(ah, bh) + _dot(ah, bl) + _dot(al, bh)


def _head_sums(x, ones):
    gw = ones.shape[0]
    parts = [_split_dot(x[:, i:i + gw], ones) for i in range(0, x.shape[1], gw)]
    return jnp.concatenate(parts, axis=1)


def _rwkv_prep_kernel(h_ref, hp_ref, mu_ref, w0_ref, a0_ref, kk_ref, ka_ref, rk_ref, wwa_ref,
                      gup_ref, ones_ref, at_out, bt_out, kt_out, rt_out, bh_out, kh_out, v_out,
                      g_out, bvg_out, wc_out, *, blocks_per_seq, aw, chunk):
    i = pl.program_id(0)
    h = h_ref[...]
    tm = h.shape[0]
    first = (i % blocks_per_seq) == 0
    prev_row = jnp.where(first, 0.0, hp_ref[7:8, :])
    shifted = pltpu.roll(h, 1, axis=0)
    row = lax.broadcasted_iota(jnp.int32, h.shape, 0)
    shifted = jnp.where(row == 0, prev_row, shifted)
    hm = h + (shifted - h) * mu_ref[...]
    r = hm[:, 0:aw]
    k = hm[:, aw:2 * aw]
    v = hm[:, 2 * aw:3 * aw]
    z = hm[:, 3 * aw:3 * aw + 128]
    lane = lax.broadcasted_iota(jnp.int32, z.shape, 1)
    z = jnp.where(lane < 64, jnp.tanh(z), z)
    wa = _dot3(z, wwa_ref[...])
    w = -jax.nn.softplus(-(w0_ref[...] + wa[:, :aw])) - 0.5
    a = jax.nn.sigmoid(a0_ref[...] + wa[:, aw:])
    lg = hm[:, 3 * aw + 128:]
    g = _dot3(jax.nn.sigmoid(lg), gup_ref[...])
    lw = -jnp.exp(w)

    ri = lax.broadcasted_iota(jnp.int32, (tm, tm), 0)
    ci = lax.broadcasted_iota(jnp.int32, (tm, tm), 1)
    shift = int(math.log2(chunk))
    same = (ri >> shift) == (ci >> shift)
    lw_parts = _bf16_parts(lw, 3)
    tri = jnp.where(same & (ri >= ci), 1.0, 0.0).astype(BF16)
    blk = jnp.where(same, 1.0, 0.0).astype(BF16)
    cs = sum(_dot(tri, part) for part in lw_parts)
    tot = sum(_dot(blk, part) for part in lw_parts)

    ones = ones_ref[...]
    kkr = k * kk_ref[...]
    kk = kkr / jnp.maximum(jnp.sqrt(_head_sums(kkr * kkr, ones)), 1e-12)
    k2 = k * (1.0 + (a - 1.0) * ka_ref[...])
    bb = kk * a
    w_inv = jnp.exp(-cs)
    w_rem = jnp.exp(tot - cs)
    at_out[...] = (-kk * jnp.exp(cs - lw)).astype(BF16)
    bt_out[...] = (bb * w_inv).astype(BF16)
    kt_out[...] = (k2 * w_inv).astype(BF16)
    rt_out[...] = (r * jnp.exp(cs)).astype(BF16)
    bh_out[...] = (bb * w_rem).astype(BF16)
    kh_out[...] = (k2 * w_rem).astype(BF16)
    v_out[...] = v.astype(BF16)
    g_out[...] = g
    bvg_out[...] = _head_sums(r * k2 * rk_ref[...], ones) * v * g
    for c in range(tm // chunk):
        wc_out[c] = jnp.exp(tot[c * chunk:c * chunk + 1, :])


def _rwkv_prep(h, mu, w0, a0, k_k, k_a, r_k, wwa, gup, ones, seq, tm):
    t, cols = h.shape
    aw = w0.shape[1]
    chunk = RWKV_CHUNK
    row_spec = pl.BlockSpec((tm, aw), lambda i: (i, 0))
    full = lambda arr: pl.BlockSpec(arr.shape, lambda i: (0, 0))
    kern = functools.partial(_rwkv_prep_kernel, blocks_per_seq=seq // tm, aw=aw, chunk=chunk)
    return pl.pallas_call(
        kern,
        grid=(t // tm,),
        in_specs=[pl.BlockSpec((tm, cols), lambda i: (i, 0)),
                  pl.BlockSpec((8, cols), lambda i: (jnp.maximum(i * (tm // 8) - 1, 0), 0)),
                  full(mu), full(w0), full(a0), full(k_k), full(k_a), full(r_k), full(wwa),
                  full(gup), full(ones)],
        out_specs=[row_spec] * 9 + [pl.BlockSpec((tm // chunk, 1, aw), lambda i: (i, 0, 0))],
        out_shape=([jax.ShapeDtypeStruct((t, aw), BF16)] * 7 + [jax.ShapeDtypeStruct((t, aw), F32)] * 2
                   + [jax.ShapeDtypeStruct((t // chunk, 1, aw), F32)]),
        compiler_params=_params("parallel"),
        name="rwkv_prep",
    )(h, h, mu, w0, a0, k_k, k_a, r_k, wwa, gup, ones)


def _rwkv_scan_kernel(at_ref, bt_ref, kt_ref, rt_ref, bh_ref, kh_ref, v_ref, wc_ref, y_ref, st_ref,
                      *, pairs, n):
    c = n
    n2 = 2 * n
    nsub = at_ref.shape[0] // c

    @pl.when(pl.program_id(1) == 0)
    def _():
        st_ref[...] = jnp.zeros_like(st_ref)

    row = lax.broadcasted_iota(jnp.int32, (n2, n2), 0)
    col = lax.broadcasted_iota(jnp.int32, (n2, n2), 1)
    same = (row >= c) == (col >= c)
    strict = same & (row > col)
    incl = same & (row >= col)
    incl2 = jnp.concatenate([incl, incl], axis=1)
    eye = (row == col).astype(F32)
    lo = lax.broadcasted_iota(jnp.int32, (c, n2), 1) < n

    def stack(ref, q, sl):
        x = ref[q * c:(q + 1) * c, sl]
        zero = jnp.zeros_like(x)
        return jnp.concatenate([jnp.where(lo, x, zero), jnp.where(lo, zero, x)], axis=0)

    sls = [slice(p * n2, (p + 1) * n2) for p in range(pairs)]
    units = [(q, p) for q in range(nsub) for p in range(pairs)]
    stacks = {u: [stack(ref, u[0], sls[u[1]])
                  for ref in (at_ref, bt_ref, kt_ref, rt_ref, bh_ref, kh_ref, v_ref)] for u in units}
    ar = {u: jnp.concatenate([stacks[u][0], stacks[u][3]], axis=0) for u in units}
    big = {u: _dot_nt(ar[u], jnp.concatenate([stacks[u][1], stacks[u][2]], axis=0)) for u in units}
    l_ab = {u: jnp.where(strict, big[u][:n2, :n2], 0.0) for u in units}
    l_ak = {u: jnp.where(strict, big[u][:n2, n2:], 0.0).astype(BF16) for u in units}
    m_all = {u: jnp.where(incl2, big[u][n2:, :], 0.0).astype(BF16) for u in units}

    tinv = {u: eye + l_ab[u] for u in units}
    pw = {u: l_ab[u].astype(BF16) for u in units}
    for _ in range(int(math.log2(c)) - 1):
        pw = {u: _dot(pw[u], pw[u]).astype(BF16) for u in units}
        tinv = {u: tinv[u] + _dot(tinv[u].astype(BF16), pw[u]) for u in units}
    tinv = {u: tinv[u].astype(BF16) for u in units}
    lv = {u: _dot(l_ak[u], stacks[u][6]) for u in units}

    state = [st_ref[p] for p in range(pairs)]
    for q in range(nsub):
        us = [(q, p) for p in range(pairs)]
        from_state = [_dot_nt(ar[u], state[u[1]].astype(BF16)) for u in us]
        x1 = [(from_state[p][:n2] + lv[us[p]]).astype(BF16) for p in range(pairs)]
        uu = [_dot(tinv[us[p]], x1[p]).astype(BF16) for p in range(pairs)]
        uv = [jnp.concatenate([uu[p], stacks[us[p]][6]], axis=0) for p in range(pairs)]
        for p in range(pairs):
            u = us[p]
            yst = from_state[p][n2:] + _dot(m_all[u], uv[p])
            state[p] = state[p] * wc_ref[q, :, sls[p]] + _dot_tn(
                uv[p], jnp.concatenate([stacks[u][4], stacks[u][5]], axis=0))
            y_ref[q * c:(q + 1) * c, sls[p]] = yst[:c] + yst[c:]
    for p in range(pairs):
        st_ref[p] = state[p]


def _rwkv_scan(at, bt, kt, rt, bh, kh, v, wc, batch, seq):
    t, aw = at.shape
    c = RWKV_CHUNK
    n = HEAD_DIM
    assert c == n
    pairs = aw // (2 * n)
    nsub = RWKV_CHUNKS_PER_STEP
    nstep = seq // (c * nsub)
    tok = pl.BlockSpec((c * nsub, aw), lambda b, j: (b * nstep + j, 0))
    kern = functools.partial(_rwkv_scan_kernel, pairs=pairs, n=n)
    return pl.pallas_call(
        kern,
        grid=(batch, nstep),
        in_specs=[tok] * 7 + [pl.BlockSpec((nsub, 1, aw), lambda b, j: (b * nstep + j, 0, 0))],
        out_specs=tok,
        out_shape=jax.ShapeDtypeStruct((t, aw), F32),
        scratch_shapes=[pltpu.VMEM((pairs, 2 * n, 2 * n), F32)],
        compiler_params=_params("parallel", "arbitrary"),
        name="rwkv_scan",
    )(at, bt, kt, rt, bh, kh, v, wc)


def _rope_tables(positions, head_dim, lanes):
    rd = head_dim // ROPE_FRACTION
    half = rd // 2
    inv_freq = ROPE_THETA ** (-jnp.arange(half, dtype=F32) / half)
    ang = positions.astype(F32).reshape(-1, 1) * inv_freq
    cos, sin = jnp.cos(ang), jnp.sin(ang)
    t = ang.shape[0]
    pad = jnp.zeros((t, head_dim - rd), F32)
    zero = jnp.zeros((t, half), F32)
    c = jnp.concatenate([cos, cos, pad + 1.0], -1)
    sn = jnp.concatenate([-sin, zero, pad], -1)
    sp = jnp.concatenate([zero, sin, pad], -1)
    rep = lanes // head_dim
    return tuple(jnp.tile(x, (1, rep)) for x in (c, sn, sp)), half


def _rope(x, c, sn, sp, half):
    period = c.shape[1]
    slabs = []
    for g in range(x.shape[1] // period):
        xg = x[:, g * period:(g + 1) * period]
        nxt = pltpu.roll(xg, period - half, axis=1)
        prv = pltpu.roll(xg, half, axis=1)
        slabs.append(xg * c + nxt * sn + prv * sp)
    return slabs[0] if len(slabs) == 1 else jnp.concatenate(slabs, axis=1)


def _swa_kernel(sink_ref, q_ref, kvp_ref, kvc_ref, c_ref, sn_ref, sp_ref, cp_ref, snp_ref, spp_ref,
                o_ref, *, q_heads, n, half, scale):
    blk = q_ref.shape[0]
    j = pl.program_id(1)
    n2 = 2 * n
    pairs = q_heads // 2
    tabs = (c_ref[...], sn_ref[...], sp_ref[...])
    q = (_rope(q_ref[...], *tabs, half) * scale).astype(BF16)
    kvp, kvc = kvp_ref[...], kvc_ref[...]
    k2 = jnp.concatenate([_rope(kvp[:, :n2], cp_ref[...], snp_ref[...], spp_ref[...], half),
                          _rope(kvc[:, :n2], *tabs, half)], axis=0)
    v2 = jnp.concatenate([kvp[:, n2:], kvc[:, n2:]], axis=0)
    lo2 = lax.broadcasted_iota(jnp.int32, (2 * blk, n2), 1) < n
    dup = lambda x: [jnp.where(lo2, x, pltpu.roll(x, n, axis=1)).astype(BF16),
                     jnp.where(lo2, pltpu.roll(x, n, axis=1), x).astype(BF16)]
    kd, vd = dup(k2), dup(v2)

    qi = lax.broadcasted_iota(jnp.int32, (2 * blk, 2 * blk), 0) % blk
    kj = lax.broadcasted_iota(jnp.int32, (2 * blk, 2 * blk), 1)
    rel = qi + blk - kj
    mask = (rel >= 0) & (rel < WINDOW) & ((kj >= blk) | (j > 0))
    upper = lax.broadcasted_iota(jnp.int32, (2 * blk, 1), 0) < blk
    lo = lax.broadcasted_iota(jnp.int32, (blk, n2), 1) < n
    zero = jnp.zeros((blk, n2), BF16)
    per_group = pairs // (k2.shape[1] // n)

    ss = []
    for p in range(pairs):
        qp = q[:, p * n2:(p + 1) * n2]
        lhs = jnp.concatenate([jnp.where(lo, qp, zero), jnp.where(lo, zero, qp)], axis=0)
        ss.append(_dot_nt(lhs, kd[p // per_group]))
    ps = []
    for p in range(pairs):
        s = jnp.where(mask, ss[p], -jnp.inf)
        sink = jnp.where(upper, sink_ref[2 * p], sink_ref[2 * p + 1])
        m = jnp.maximum(jnp.max(s, -1, keepdims=True), sink)
        e = jnp.exp(s - m)
        den = jnp.sum(e, -1, keepdims=True) + jnp.exp(sink - m)
        ps.append((e / den).astype(BF16))
    for p in range(pairs):
        o2 = _dot(ps[p], vd[p // per_group])
        o_ref[:, p * n2:(p + 1) * n2] = jnp.where(lo, o2[:blk], o2[blk:]).astype(o_ref.dtype)


def _swa(hb, tabs, half, sinks, batch, seq):
    t = hb.shape[0]
    n = HEAD_DIM
    qw = sinks.shape[0] * n
    kvw2 = hb.shape[1] - qw
    blk = WINDOW
    nb = seq // blk
    kv_col = qw // kvw2
    cur = lambda b, j: (b * nb + j, 0)
    prev = lambda b, j: (b * nb + jnp.maximum(j - 1, 0), 0)
    cur_kv = lambda b, j: (b * nb + j, kv_col)
    prev_kv = lambda b, j: (b * nb + jnp.maximum(j - 1, 0), kv_col)
    kern = functools.partial(_swa_kernel, q_heads=qw // n, n=n, half=half, scale=n ** -0.5)
    tab = lambda idx: pl.BlockSpec((blk, 128), idx)
    return pl.pallas_call(
        kern,
        grid=(batch, nb),
        in_specs=[pl.BlockSpec(memory_space=pltpu.SMEM),
                  pl.BlockSpec((blk, qw), cur),
                  pl.BlockSpec((blk, kvw2), prev_kv), pl.BlockSpec((blk, kvw2), cur_kv),
                  tab(cur), tab(cur), tab(cur), tab(prev), tab(prev), tab(prev)],
        out_specs=pl.BlockSpec((blk, qw), cur),
        out_shape=jax.ShapeDtypeStruct((t, qw), F32),
        compiler_params=_params("parallel", "arbitrary"),
        name="swa",
    )(sinks, hb, hb, hb, *tabs, *tabs)


def _diff_rope_kernel(x_ref, c_ref, sn_ref, sp_ref, o_ref, *, half):
    x = x_ref[...].astype(F32)
    o_ref[...] = _rope(x, c_ref[...], sn_ref[...], sp_ref[...], half).astype(o_ref.dtype)


def _diff_rope_k(h, tabs, half, width, tm, tn):
    t = h.shape[0]
    first = width // tn
    tab_spec = pl.BlockSpec((tm, 128), lambda i, j: (i, 0))
    return pl.pallas_call(
        functools.partial(_diff_rope_kernel, half=half),
        grid=(t // tm, width // tn),
        in_specs=[pl.BlockSpec((tm, tn), lambda i, j: (i, first + j)), tab_spec, tab_spec, tab_spec],
        out_specs=pl.BlockSpec((tm, tn), lambda i, j: (i, j)),
        out_shape=jax.ShapeDtypeStruct((t, width), h.dtype),
        compiler_params=_params("parallel", "arbitrary"),
        name="diff_rope_k",
    )(h, *tabs)


def _diff_attn_kernel(sc_ref, q_ref, c_ref, sn_ref, sp_ref, k_ref, v_ref, g_ref, o_ref, m_ref, l_ref,
                      acc_ref, qs_ref, *, blk, rg, dh, half):
    qi = pl.program_id(2)
    qs_ref[...] = _rope(q_ref[...].astype(F32), c_ref[...], sn_ref[...], sp_ref[...],
                        half).astype(qs_ref.dtype)
    groups = blk // rg
    vrep = 2 * dh // 128

    def block(j, diagonal, first):
        off = pl.multiple_of(j * blk, blk)
        nkeys = [(r + 1) * rg if diagonal else blk for r in range(groups)]
        rows = [slice(r * rg, (r + 1) * rg) for r in range(groups)]
        ss = [[_dot_nt(qs_ref[rows[r], c * dh:(c + 1) * dh],
                       k_ref[pl.ds(off, nkeys[r]), c * dh:(c + 1) * dh])
               for c in range(2)] for r in range(groups)]
        ps, alphas = [], []
        for r, rs in enumerate(rows):
            if diagonal:
                row = lax.broadcasted_iota(jnp.int32, (rg, nkeys[r]), 0) + r * rg
                col = lax.broadcasted_iota(jnp.int32, (rg, nkeys[r]), 1)
                ss[r] = [jnp.where(col <= row, s, -jnp.inf) for s in ss[r]]
            pr, ar = [], []
            for c in range(2):
                top = jnp.max(ss[r][c], -1, keepdims=True)
                if first:
                    m_new = jnp.broadcast_to(top, (rg, 128))
                else:
                    m_old = m_ref[c, rs, :]
                    m_new = jnp.maximum(m_old, top)
                    alpha = jnp.exp2(m_old - m_new)
                p = jnp.exp2(ss[r][c] - jnp.concatenate([m_new] * (nkeys[r] // 128), axis=1))
                psum = jnp.sum(p, -1, keepdims=True)
                if first:
                    l_ref[c, rs, :] = jnp.broadcast_to(psum, (rg, 128))
                else:
                    l_ref[c, rs, :] = alpha * l_ref[c, rs, :] + psum
                    ar.append(jnp.concatenate([alpha] * vrep, axis=1))
                m_ref[c, rs, :] = m_new
                pr.append(p.astype(v_ref.dtype))
            ps.append(pr)
            alphas.append(ar)
        v_all = v_ref[pl.ds(off, blk), :]
        for r, rs in enumerate(rows):
            v = v_all[:nkeys[r]]
            for c in range(2):
                pv = _dot(ps[r][c], v)
                acc_ref[c, rs, :] = pv if first else alphas[r][c] * acc_ref[c, rs, :] + pv

    @pl.when(qi == 0)
    def _():
        block(0, True, True)

    @pl.when(qi > 0)
    def _():
        block(0, False, True)

        def body(j, carry):
            block(j, False, False)
            return carry

        lax.fori_loop(1, qi, body, 0)
        block(qi, True, False)

    lam = sc_ref[0]
    post = sc_ref[1]
    o1 = acc_ref[0] / jnp.concatenate([l_ref[0]] * vrep, axis=1)
    o2 = acc_ref[1] / jnp.concatenate([l_ref[1]] * vrep, axis=1)
    out = o1 - lam * o2
    out = out * lax.rsqrt(jnp.mean(out * out, -1, keepdims=True) + SUBLN_EPS)
    o_ref[...] = (out * g_ref[...] * post).astype(o_ref.dtype)


def _diff_attn(h, k_rot, tabs, half, scalars, subln_g, batch, seq, blk, rg):
    t, width = k_rot.shape
    dh = DIFF_HEAD_DIM
    hw = 2 * dh
    heads = width // hw
    nq = seq // blk
    kern = functools.partial(_diff_attn_kernel, blk=blk, rg=rg, dh=dh, half=half)
    resident = lambda first, bufs: pl.BlockSpec((seq, hw), lambda b, h, qi: (b, first + h),
                                                pipeline_mode=pl.Buffered(bufs))
    tab_spec = pl.BlockSpec((blk, 128), lambda b, h, qi: (b * nq + qi, 0))
    return pl.pallas_call(
        kern,
        grid=(batch, heads, nq),
        in_specs=[pl.BlockSpec(memory_space=pltpu.SMEM),
                  pl.BlockSpec((blk, hw), lambda b, h, qi: (b * nq + qi, h)),
                  tab_spec, tab_spec, tab_spec,
                  resident(0, 2), resident(2 * heads, 1),
                  pl.BlockSpec((1, hw), lambda b, h, qi: (0, 0))],
        out_specs=pl.BlockSpec((blk, hw), lambda b, h, qi: (b * nq + qi, h)),
        out_shape=jax.ShapeDtypeStruct((t, width), BF16),
        scratch_shapes=[pltpu.VMEM((2, blk, 128), F32), pltpu.VMEM((2, blk, 128), F32),
                        pltpu.VMEM((2, blk, hw), F32), pltpu.VMEM((blk, hw), BF16)],
        compiler_params=_params("parallel", "parallel", "arbitrary"),
        name="diff_attn",
    )(scalars, h, *tabs, k_rot, h, subln_g)


def _rwkv_finish(y, g, bvg, gnw, gnb, ones, n):
    ym = _head_sums(y, ones) * (1.0 / n)
    d = y - ym
    yv = _head_sums(d * d, ones) * (1.0 / n)
    return (d * lax.rsqrt(yv + WKV_GN_EPS) * gnw + gnb) * g + bvg


def _even_outproj_ln_kernel(y_ref, g_ref, bvg_ref, gnw_ref, gnb_ref, ones_ref, yb_ref, wa_ref, wb_ref,
                            x_ref, lg_ref, lb_ref, o_ref, obf_ref, *, alpha, n):
    ya = _rwkv_finish(y_ref[...], g_ref[...], bvg_ref[...], gnw_ref[...], gnb_ref[...],
                      ones_ref[...], n)
    mix = _dot(ya.astype(BF16), wa_ref[...]) + _dot(yb_ref[...].astype(BF16), wb_ref[...])
    out = _layer_norm(alpha * x_ref[...] + mix, lg_ref[...], lb_ref[...])
    o_ref[...] = out
    obf_ref[...] = out.astype(BF16)


def _even_outproj_ln(y, g, bvg, gn_w, gn_b, ones, yb, wa, wb, x, lg, lb, alpha, tm):
    t, d = x.shape
    kern = functools.partial(_even_outproj_ln_kernel, alpha=alpha, n=HEAD_DIM)
    row = lambda arr: pl.BlockSpec((tm, arr.shape[1]), lambda i: (i, 0))
    full = lambda arr: pl.BlockSpec(arr.shape, lambda i: (0, 0), pipeline_mode=pl.Buffered(1))
    return pl.pallas_call(
        kern,
        grid=(t // tm,),
        in_specs=[row(y), row(g), row(bvg), full(gn_w), full(gn_b), full(ones), row(yb), full(wa),
                  full(wb), row(x), full(lg), full(lb)],
        out_specs=[row(x), row(x)],
        out_shape=[jax.ShapeDtypeStruct((t, d), F32), jax.ShapeDtypeStruct((t, d), BF16)],
        compiler_params=_params("parallel"),
        name="even_outproj_ln",
    )(y, g, bvg, gn_w, gn_b, ones, yb, wa, wb, x, lg, lb)


def _outproj_ln_kernel(*refs, n_in, alpha):
    ys = refs[:n_in]
    ws = refs[n_in:2 * n_in]
    x_ref, g_ref, b_ref, o_ref, obf_ref = refs[2 * n_in:]
    mix = _dot(ys[0][...].astype(BF16), ws[0][...])
    for y, w in zip(ys[1:], ws[1:]):
        mix = mix + _dot(y[...].astype(BF16), w[...])
    out = _layer_norm(alpha * x_ref[...] + mix, g_ref[...], b_ref[...])
    o_ref[...] = out
    obf_ref[...] = out.astype(BF16)


def _outproj_ln(ys, ws, x, g, b, alpha, tm):
    t, d = x.shape
    n_in = len(ys)
    kern = functools.partial(_outproj_ln_kernel, n_in=n_in, alpha=alpha)
    row = lambda width: pl.BlockSpec((tm, width), lambda i: (i, 0))
    full = lambda arr: pl.BlockSpec(arr.shape, lambda i: (0, 0), pipeline_mode=pl.Buffered(1))
    return pl.pallas_call(
        kern,
        grid=(t // tm,),
        in_specs=[row(y.shape[1]) for y in ys] + [full(w) for w in ws] + [row(d), full(g), full(b)],
        out_specs=[row(d), row(d)],
        out_shape=[jax.ShapeDtypeStruct((t, d), F32), jax.ShapeDtypeStruct((t, d), BF16)],
        compiler_params=_params("parallel"),
        name="outproj_ln",
    )(*ys, *ws, x, g, b)


def _ffn_up_kernel(x_ref, w1_ref, w3_ref, o_ref):
    x = x_ref[...]
    h1 = _dot(x, w1_ref[...].astype(BF16))
    h3 = _dot(x, w3_ref[...].astype(BF16))
    o_ref[...] = (h1 * jax.nn.sigmoid(h1) * h3).astype(o_ref.dtype)


def _ffn_up(x, w1, w3, layer, tm, tf):
    t, d = x.shape
    f = w1.shape[2]
    return pl.pallas_call(
        _ffn_up_kernel,
        grid=(t // tm, f // tf),
        in_specs=[pl.BlockSpec((tm, d), lambda i, j: (i, 0)),
                  pl.BlockSpec((None, d, tf), lambda i, j: (layer, 0, j)),
                  pl.BlockSpec((None, d, tf), lambda i, j: (layer, 0, j))],
        out_specs=pl.BlockSpec((tm, tf), lambda i, j: (i, j)),
        out_shape=jax.ShapeDtypeStruct((t, f), BF16),
        compiler_params=_params("parallel", "arbitrary"),
        name="ffn_up",
    )(x, w1, w3)


def _ffn_down_ple_kernel(a_ref, w2_ref, x_ref, g_ref, b_ref, p_ref, wg_ref, wp_ref, o_ref, obf_ref,
                         *, alpha):
    y = _dot(a_ref[...], w2_ref[...])
    x2 = _layer_norm(alpha * x_ref[...] + y, g_ref[...], b_ref[...])
    gate = jax.nn.sigmoid(_dot(x2.astype(BF16), wg_ref[...]))
    proj = _dot(p_ref[...].astype(BF16), wp_ref[...])
    out = x2 + gate * proj
    o_ref[...] = out
    obf_ref[...] = out.astype(BF16)


def _ffn_down_ple(act, w2, x, g, b, p, layer, wg, wp, alpha, tm):
    t, d = x.shape
    f = act.shape[1]
    pd = p.shape[1]
    p_first = layer * (t // tm)
    kern = functools.partial(_ffn_down_ple_kernel, alpha=alpha)
    row = lambda width: pl.BlockSpec((tm, width), lambda i: (i, 0))
    once = lambda arr: pl.BlockSpec(arr.shape, lambda i: (0, 0), pipeline_mode=pl.Buffered(1))
    return pl.pallas_call(
        kern,
        grid=(t // tm,),
        in_specs=[row(f), once(w2), row(d), once(g), once(b),
                  pl.BlockSpec((tm, pd), lambda i: (p_first + i, 0)), once(wg), once(wp)],
        out_specs=[row(d), row(d)],
        out_shape=[jax.ShapeDtypeStruct((t, d), F32), jax.ShapeDtypeStruct((t, d), BF16)],
        compiler_params=_params("parallel"),
        name="ffn_down_ple",
    )(act, w2, x, g, b, p, wg, wp)


def _even_mixer(xbf, positions, w_in, mu, w0, w_up, a0, a_up, g_up, k_k, k_a, r_k, gn_w, gn_b,
                sinks, batch, seq):
    aw = w0.shape[0]
    dl = w_up.shape[0]
    al = a_up.shape[0]
    gl = g_up.shape[0]
    a_cols = 3 * aw + dl + al + gl
    gl_pad = 384
    assert dl == 64 and al == 64 and gl <= gl_pad
    d = w_in.shape[0]
    pad_cols = jnp.zeros((d, gl_pad - gl), F32)
    w_a = jnp.concatenate([w_in[:, :a_cols], pad_cols], axis=1).astype(BF16)
    w_b = w_in[:, a_cols:].astype(BF16)
    mu_p = jnp.concatenate([mu, jnp.zeros((gl_pad - gl,), F32)])[None, :]
    zeros = jnp.zeros((dl, aw), F32)
    wwa = jnp.concatenate([jnp.concatenate([w_up, zeros], 1), jnp.concatenate([zeros, a_up], 1)], 0)
    gup = jnp.concatenate([g_up, jnp.zeros((gl_pad - gl, aw), F32)], 0)

    h_a = _matmul(xbf, w_a, F32, 2048, 512, "proj_rwkv")
    h_b = _matmul(xbf, w_b, F32, 1024, w_b.shape[1], "proj_swa")

    ones = jnp.kron(jnp.eye(2, dtype=F32), jnp.ones((HEAD_DIM, HEAD_DIM), F32)).astype(BF16)
    at, bt, kt, rt, bh, kh, v, g, bvg, wc = _rwkv_prep(
        h_a, mu_p, w0[None], a0[None], k_k[None], k_a[None], r_k.reshape(1, aw), wwa, gup, ones,
        seq, 256)
    y = _rwkv_scan(at, bt, kt, rt, bh, kh, v, wc, batch, seq)

    tabs, half = _rope_tables(positions, HEAD_DIM, 128)
    yb = _swa(h_b, tabs, half, sinks, batch, seq)
    return (y, g, bvg, gn_w[None], gn_b[None], ones), yb


def _odd_mixer(xbf, positions, w_in, lam_p, subln_g, layer_idx, batch, seq):
    width = w_in.shape[1] // 3
    q_scale = DIFF_HEAD_DIM ** -0.5 * math.log2(math.e)
    w = jnp.concatenate([w_in[:, :width] * q_scale, w_in[:, width:]], axis=1).astype(BF16)
    h = _matmul(xbf, w, BF16, 2048, 1024, "proj_diff")
    tabs, half = _rope_tables(positions, DIFF_HEAD_DIM, 128)
    k_rot = _diff_rope_k(h, tabs, half, width, 1024, 1024)
    lam_init = 0.8 - 0.6 * math.exp(-0.3 * layer_idx)
    lv = lam_p.astype(F32)
    lam = jnp.exp(jnp.sum(lv[0] * lv[1])) - jnp.exp(jnp.sum(lv[2] * lv[3])) + lam_init
    scalars = jnp.stack([lam, jnp.asarray(1.0 - lam_init, F32)]).astype(F32)
    return _diff_attn(h, k_rot, tabs, half, scalars, subln_g[None], batch, seq, 1024, 256)


def kernel(x, p, positions, e_w_in, e_mu, e_w0, e_w_up, e_a0, e_a_up, e_g_up, e_k_k, e_k_a, e_r_k,
           e_gn_w, e_gn_b, e_sinks, e_w_out, o_w_in, o_lambda, o_subln_g, o_w_out,
           ln1_g, ln1_b, ln2_g, ln2_b, ffn_w1, ffn_w3, ffn_w2, ple_w_proj, ple_w_gate):
    batch, seq, d = x.shape
    depth = p.shape[0]
    t = batch * seq
    alpha = (2.0 * depth) ** 0.25
    xf = x.reshape(t, d)
    xbf = xf.astype(BF16)
    p_rows = p.reshape(depth * t, p.shape[-1])
    for i in range(depth):
        j = i // 2
        if i % 2 == 0:
            ya, yb = _even_mixer(xbf, positions, e_w_in[j], e_mu[j], e_w0[j], e_w_up[j], e_a0[j],
                                 e_a_up[j], e_g_up[j], e_k_k[j], e_k_a[j], e_r_k[j], e_gn_w[j],
                                 e_gn_b[j], e_sinks[j], batch, seq)
            aw = ya[0].shape[1]
            w_out = e_w_out[j].astype(BF16)
            xf, xbf = _even_outproj_ln(*ya, yb, w_out[:aw], w_out[aw:], xf, ln1_g[i][None],
                                       ln1_b[i][None], alpha, 512)
        else:
            y = _odd_mixer(xbf, positions, o_w_in[j], o_lambda[j], o_subln_g[j], i, batch, seq)
            xf, xbf = _outproj_ln([y], [o_w_out[j].astype(BF16)], xf, ln1_g[i][None],
                                  ln1_b[i][None], alpha, 512)
        act = _ffn_up(xbf, ffn_w1, ffn_w3, i, 1024, 512)
        xf, xbf = _ffn_down_ple(act, ffn_w2[i].astype(BF16), xf, ln2_g[i][None], ln2_b[i][None],
                                p_rows, i, ple_w_gate[i].astype(BF16),
                                ple_w_proj[i].astype(BF16), alpha, 256)
    return xf.reshape(batch, seq, d)
```

```python
import functools
import math

import jax
import jax.numpy as jnp
from jax import lax
from jax.experimental import pallas as pl
from jax.experimental.pallas import tpu as pltpu

F32 = jnp.float32
BF16 = jnp.bfloat16
HI = lax.Precision.HIGHEST

HEAD_DIM = 64
DIFF_HEAD_DIM = 128
WINDOW = 128
ROPE_THETA = 500000.0
ROPE_FRACTION = 4
LN_EPS = 1e-5
WKV_GN_EPS = 64e-5
SUBLN_EPS = 1e-5
RWKV_CHUNK = 64
RWKV_CHUNKS_PER_STEP = 4
VMEM_LIMIT = 56 * 1024 * 1024


def _dot(a, b, prec=None):
    return jnp.dot(a, b, preferred_element_type=F32, precision=prec)


def _dot_nt(a, b, prec=None):
    return lax.dot_general(a, b, (((1,), (1,)), ((), ())), preferred_element_type=F32, precision=prec)


def _dot_tn(a, b, prec=None):
    return lax.dot_general(a, b, (((0,), (0,)), ((), ())), preferred_element_type=F32, precision=prec)


def _params(*sem):
    return pltpu.CompilerParams(dimension_semantics=sem, vmem_limit_bytes=VMEM_LIMIT)


def _layer_norm(z, g, b):
    mu = jnp.mean(z, -1, keepdims=True)
    d = z - mu
    var = jnp.mean(d * d, -1, keepdims=True)
    return d * lax.rsqrt(var + LN_EPS) * g + b


def _mm_kernel(x_ref, w_ref, o_ref):
    o_ref[...] = _dot(x_ref[...], w_ref[...]).astype(o_ref.dtype)


def _matmul(x, w, out_dtype, tm, tn, name):
    t, k = x.shape
    n = w.shape[1]
    return pl.pallas_call(
        _mm_kernel,
        grid=(t // tm, n // tn),
        in_specs=[pl.BlockSpec((tm, k), lambda i, j: (i, 0)),
                  pl.BlockSpec((k, tn), lambda i, j: (0, j))],
        out_specs=pl.BlockSpec((tm, tn), lambda i, j: (i, j)),
        out_shape=jax.ShapeDtypeStruct((t, n), out_dtype),
        compiler_params=_params("parallel", "arbitrary"),
        name=name,
    )(x, w)


def _bf16_parts(x, n):
    parts = []
    for _ in range(n):
        p = x.astype(BF16)
        parts.append(p)
        x = x - p.astype(F32)
    return parts


def _split_dot(x, w, n=2):
    return sum(_dot(p, w) for p in _bf16_parts(x, n))


def _dot3(a, b):
    ah, al = _bf16_parts(a, 2)
    bh, bl = _bf16_parts(b, 2)
    return _dot(ah, bh) + _dot(ah, bl) + _dot(al, bh)


def _head_sums(x, ones):
    gw = ones.shape[0]
    parts = [_split_dot(x[:, i:i + gw], ones) for i in range(0, x.shape[1], gw)]
    return jnp.concatenate(parts, axis=1)


def _rwkv_prep_kernel(h_ref, hp_ref, mu_ref, w0_ref, a0_ref, kk_ref, ka_ref, rk_ref, wwa_ref,
                      gup_ref, ones_ref, at_out, bt_out, kt_out, rt_out, bh_out, kh_out, v_out,
                      g_out, bvg_out, wc_out, *, blocks_per_seq, aw, chunk):
    i = pl.program_id(0)
    h = h_ref[...]
    tm = h.shape[0]
    first = (i % blocks_per_seq) == 0
    prev_row = jnp.where(first, 0.0, hp_ref[7:8, :])
    shifted = pltpu.roll(h, 1, axis=0)
    row = lax.broadcasted_iota(jnp.int32, h.shape, 0)
    shifted = jnp.where(row == 0, prev_row, shifted)
    hm = h + (shifted - h) * mu_ref[...]
    r = hm[:, 0:aw]
    k = hm[:, aw:2 * aw]
    v = hm[:, 2 * aw:3 * aw]
    z = hm[:, 3 * aw:3 * aw + 128]
    lane = lax.broadcasted_iota(jnp.int32, z.shape, 1)
    z = jnp.where(lane < 64, jnp.tanh(z), z)
    wa = _dot3(z, wwa_ref[...])
    w = -jax.nn.softplus(-(w0_ref[...] + wa[:, :aw])) - 0.5
    a = jax.nn.sigmoid(a0_ref[...] + wa[:, aw:])
    lg = hm[:, 3 * aw + 128:]
    g = _dot3(jax.nn.sigmoid(lg), gup_ref[...])
    lw = -jnp.exp(w)

    ri = lax.broadcasted_iota(jnp.int32, (tm, tm), 0)
    ci = lax.broadcasted_iota(jnp.int32, (tm, tm), 1)
    shift = int(math.log2(chunk))
    same = (ri >> shift) == (ci >> shift)
    lw_parts = _bf16_parts(lw, 3)
    tri = jnp.where(same & (ri >= ci), 1.0, 0.0).astype(BF16)
    blk = jnp.where(same, 1.0, 0.0).astype(BF16)
    cs = sum(_dot(tri, part) for part in lw_parts)
    tot = sum(_dot(blk, part) for part in lw_parts)

    ones = ones_ref[...]
    kkr = k * kk_ref[...]
    kk = kkr / jnp.maximum(jnp.sqrt(_head_sums(kkr * kkr, ones)), 1e-12)
    k2 = k * (1.0 + (a - 1.0) * ka_ref[...])
    bb = kk * a
    w_inv = jnp.exp(-cs)
    w_rem = jnp.exp(tot - cs)
    at_out[...] = (-kk * jnp.exp(cs - lw)).astype(BF16)
    bt_out[...] = (bb * w_inv).astype(BF16)
    kt_out[...] = (k2 * w_inv).astype(BF16)
    rt_out[...] = (r * jnp.exp(cs)).astype(BF16)
    bh_out[...] = (bb * w_rem).astype(BF16)
    kh_out[...] = (k2 * w_rem).astype(BF16)
    v_out[...] = v.astype(BF16)
    g_out[...] = g
    bvg_out[...] = _head_sums(r * k2 * rk_ref[...], ones) * v * g
    for c in range(tm // chunk):
        wc_out[c] = jnp.exp(tot[c * chunk:c * chunk + 1, :])


def _rwkv_prep(h, mu, w0, a0, k_k, k_a, r_k, wwa, gup, ones, seq, tm):
    t, cols = h.shape
    aw = w0.shape[1]
    chunk = RWKV_CHUNK
    row_spec = pl.BlockSpec((tm, aw), lambda i: (i, 0))
    full = lambda arr: pl.BlockSpec(arr.shape, lambda i: (0, 0))
    kern = functools.partial(_rwkv_prep_kernel, blocks_per_seq=seq // tm, aw=aw, chunk=chunk)
    return pl.pallas_call(
        kern,
        grid=(t // tm,),
        in_specs=[pl.BlockSpec((tm, cols), lambda i: (i, 0)),
                  pl.BlockSpec((8, cols), lambda i: (jnp.maximum(i * (tm // 8) - 1, 0), 0)),
                  full(mu), full(w0), full(a0), full(k_k), full(k_a), full(r_k), full(wwa),
                  full(gup), full(ones)],
        out_specs=[row_spec] * 9 + [pl.BlockSpec((tm // chunk, 1, aw), lambda i: (i, 0, 0))],
        out_shape=([jax.ShapeDtypeStruct((t, aw), BF16)] * 7 + [jax.ShapeDtypeStruct((t, aw), F32)] * 2
                   + [jax.ShapeDtypeStruct((t // chunk, 1, aw), F32)]),
        compiler_params=_params("parallel"),
        name="rwkv_prep",
    )(h, h, mu, w0, a0, k_k, k_a, r_k, wwa, gup, ones)


def _rwkv_scan_kernel(at_ref, bt_ref, kt_ref, rt_ref, bh_ref, kh_ref, v_ref, wc_ref, y_ref, st_ref,
                      *, pairs, n):
    c = n
    n2 = 2 * n
    nsub = at_ref.shape[0] // c

    @pl.when(pl.program_id(1) == 0)
    def _():
        st_ref[...] = jnp.zeros_like(st_ref)

    row = lax.broadcasted_iota(jnp.int32, (n2, n2), 0)
    col = lax.broadcasted_iota(jnp.int32, (n2, n2), 1)
    same = (row >= c) == (col >= c)
    strict = same & (row > col)
    incl = same & (row >= col)
    incl2 = jnp.concatenate([incl, incl], axis=1)
    eye = (row == col).astype(F32)
    lo = lax.broadcasted_iota(jnp.int32, (c, n2), 1) < n

    def stack(ref, q, sl):
        x = ref[q * c:(q + 1) * c, sl]
        zero = jnp.zeros_like(x)
        return jnp.concatenate([jnp.where(lo, x, zero), jnp.where(lo, zero, x)], axis=0)

    sls = [slice(p * n2, (p + 1) * n2) for p in range(pairs)]
    units = [(q, p) for q in range(nsub) for p in range(pairs)]
    stacks = {u: [stack(ref, u[0], sls[u[1]])
                  for ref in (at_ref, bt_ref, kt_ref, rt_ref, bh_ref, kh_ref, v_ref)] for u in units}
    ar = {u: jnp.concatenate([stacks[u][0], stacks[u][3]], axis=0) for u in units}
    big = {u: _dot_nt(ar[u], jnp.concatenate([stacks[u][1], stacks[u][2]], axis=0)) for u in units}
    l_ab = {u: jnp.where(strict, big[u][:n2, :n2], 0.0) for u in units}
    l_ak = {u: jnp.where(strict, big[u][:n2, n2:], 0.0).astype(BF16) for u in units}
    m_all = {u: jnp.where(incl2, big[u][n2:, :], 0.0).astype(BF16) for u in units}

    tinv = {u: eye + l_ab[u] for u in units}
    pw = {u: l_ab[u].astype(BF16) for u in units}
    for _ in range(int(math.log2(c)) - 1):
        pw = {u: _dot(pw[u], pw[u]).astype(BF16) for u in units}
        tinv = {u: tinv[u] + _dot(tinv[u].astype(BF16), pw[u]) for u in units}
    tinv = {u: tinv[u].astype(BF16) for u in units}
    lv = {u: _dot(l_ak[u], stacks[u][6]) for u in units}

    state = [st_ref[p] for p in range(pairs)]
    for q in range(nsub):
        us = [(q, p) for p in range(pairs)]
        from_state = [_dot_nt(ar[u], state[u[1]].astype(BF16)) for u in us]
        x1 = [(from_state[p][:n2] + lv[us[p]]).astype(BF16) for p in range(pairs)]
        uu = [_dot(tinv[us[p]], x1[p]).astype(BF16) for p in range(pairs)]
        uv = [jnp.concatenate([uu[p], stacks[us[p]][6]], axis=0) for p in range(pairs)]
        for p in range(pairs):
            u = us[p]
            yst = from_state[p][n2:] + _dot(m_all[u], uv[p])
            state[p] = state[p] * wc_ref[q, :, sls[p]] + _dot_tn(
                uv[p], jnp.concatenate([stacks[u][4], stacks[u][5]], axis=0))
            y_ref[q * c:(q + 1) * c, sls[p]] = yst[:c] + yst[c:]
    for p in range(pairs):
        st_ref[p] = state[p]


def _rwkv_scan(at, bt, kt, rt, bh, kh, v, wc, batch, seq):
    t, aw = at.shape
    c = RWKV_CHUNK
    n = HEAD_DIM
    assert c == n
    pairs = aw // (2 * n)
    nsub = RWKV_CHUNKS_PER_STEP
    nstep = seq // (c * nsub)
    tok = pl.BlockSpec((c * nsub, aw), lambda b, j: (b * nstep + j, 0))
    kern = functools.partial(_rwkv_scan_kernel, pairs=pairs, n=n)
    return pl.pallas_call(
        kern,
        grid=(batch, nstep),
        in_specs=[tok] * 7 + [pl.BlockSpec((nsub, 1, aw), lambda b, j: (b * nstep + j, 0, 0))],
        out_specs=tok,
        out_shape=jax.ShapeDtypeStruct((t, aw), F32),
        scratch_shapes=[pltpu.VMEM((pairs, 2 * n, 2 * n), F32)],
        compiler_params=_params("parallel", "arbitrary"),
        name="rwkv_scan",
    )(at, bt, kt, rt, bh, kh, v, wc)


def _rope_tables(positions, head_dim, lanes):
    rd = head_dim // ROPE_FRACTION
    half = rd // 2
    inv_freq = ROPE_THETA ** (-jnp.arange(half, dtype=F32) / half)
    ang = positions.astype(F32).reshape(-1, 1) * inv_freq
    cos, sin = jnp.cos(ang), jnp.sin(ang)
    t = ang.shape[0]
    pad = jnp.zeros((t, head_dim - rd), F32)
    zero = jnp.zeros((t, half), F32)
    c = jnp.concatenate([cos, cos, pad + 1.0], -1)
    sn = jnp.concatenate([-sin, zero, pad], -1)
    sp = jnp.concatenate([zero, sin, pad], -1)
    rep = lanes // head_dim
    return tuple(jnp.tile(x, (1, rep)) for x in (c, sn, sp)), half


def _rope(x, c, sn, sp, half):
    period = c.shape[1]
    slabs = []
    for g in range(x.shape[1] // period):
        xg = x[:, g * period:(g + 1) * period]
        nxt = pltpu.roll(xg, period - half, axis=1)
        prv = pltpu.roll(xg, half, axis=1)
        slabs.append(xg * c + nxt * sn + prv * sp)
    return slabs[0] if len(slabs) == 1 else jnp.concatenate(slabs, axis=1)


def _swa_kernel(sink_ref, q_ref, kvp_ref, kvc_ref, c_ref, sn_ref, sp_ref, cp_ref, snp_ref, spp_ref,
                o_ref, *, q_heads, n, half, scale):
    blk = q_ref.shape[0]
    j = pl.program_id(1)
    n2 = 2 * n
    pairs = q_heads // 2
    tabs = (c_ref[...], sn_ref[...], sp_ref[...])
    q = (_rope(q_ref[...], *tabs, half) * scale).astype(BF16)
    kvp, kvc = kvp_ref[...], kvc_ref[...]
    k2 = jnp.concatenate([_rope(kvp[:, :n2], cp_ref[...], snp_ref[...], spp_ref[...], half),
                          _rope(kvc[:, :n2], *tabs, half)], axis=0)
    v2 = jnp.concatenate([kvp[:, n2:], kvc[:, n2:]], axis=0)
    lo2 = lax.broadcasted_iota(jnp.int32, (2 * blk, n2), 1) < n
    dup = lambda x: [jnp.where(lo2, x, pltpu.roll(x, n, axis=1)).astype(BF16),
                     jnp.where(lo2, pltpu.roll(x, n, axis=1), x).astype(BF16)]
    kd, vd = dup(k2), dup(v2)

    qi = lax.broadcasted_iota(jnp.int32, (2 * blk, 2 * blk), 0) % blk
    kj = lax.broadcasted_iota(jnp.int32, (2 * blk, 2 * blk), 1)
    rel = qi + blk - kj
    mask = (rel >= 0) & (rel < WINDOW) & ((kj >= blk) | (j > 0))
    upper = lax.broadcasted_iota(jnp.int32, (2 * blk, 1), 0) < blk
    lo = lax.broadcasted_iota(jnp.int32, (blk, n2), 1) < n
    zero = jnp.zeros((blk, n2), BF16)
    per_group = pairs // (k2.shape[1] // n)

    ss = []
    for p in range(pairs):
        qp = q[:, p * n2:(p + 1) * n2]
        lhs = jnp.concatenate([jnp.where(lo, qp, zero), jnp.where(lo, zero, qp)], axis=0)
        ss.append(_dot_nt(lhs, kd[p // per_group]))
    ps = []
    for p in range(pairs):
        s = jnp.where(mask, ss[p], -jnp.inf)
        sink = jnp.where(upper, sink_ref[2 * p], sink_ref[2 * p + 1])
        m = jnp.maximum(jnp.max(s, -1, keepdims=True), sink)
        e = jnp.exp(s - m)
        den = jnp.sum(e, -1, keepdims=True) + jnp.exp(sink - m)
        ps.append((e / den).astype(BF16))
    for p in range(pairs):
        o2 = _dot(ps[p], vd[p // per_group])
        o_ref[:, p * n2:(p + 1) * n2] = jnp.where(lo, o2[:blk], o2[blk:]).astype(o_ref.dtype)


def _swa(hb, tabs, half, sinks, batch, seq):
    t = hb.shape[0]
    n = HEAD_DIM
    qw = sinks.shape[0] * n
    kvw2 = hb.shape[1] - qw
    blk = WINDOW
    nb = seq // blk
    kv_col = qw // kvw2
    cur = lambda b, j: (b * nb + j, 0)
    prev = lambda b, j: (b * nb + jnp.maximum(j - 1, 0), 0)
    cur_kv = lambda b, j: (b * nb + j, kv_col)
    prev_kv = lambda b, j: (b * nb + jnp.maximum(j - 1, 0), kv_col)
    kern = functools.partial(_swa_kernel, q_heads=qw // n, n=n, half=half, scale=n ** -0.5)
    tab = lambda idx: pl.BlockSpec((blk, 128), idx)
    return pl.pallas_call(
        kern,
        grid=(batch, nb),
        in_specs=[pl.BlockSpec(memory_space=pltpu.SMEM),
                  pl.BlockSpec((blk, qw), cur),
                  pl.BlockSpec((blk, kvw2), prev_kv), pl.BlockSpec((blk, kvw2), cur_kv),
                  tab(cur), tab(cur), tab(cur), tab(prev), tab(prev), tab(prev)],
        out_specs=pl.BlockSpec((blk, qw), cur),
        out_shape=jax.ShapeDtypeStruct((t, qw), F32),
        compiler_params=_params("parallel", "arbitrary"),
        name="swa",
    )(sinks, hb, hb, hb, *tabs, *tabs)


def _diff_rope_kernel(x_ref, c_ref, sn_ref, sp_ref, o_ref, *, half):
    x = x_ref[...].astype(F32)
    o_ref[...] = _rope(x, c_ref[...], sn_ref[...], sp_ref[...], half).astype(o_ref.dtype)


def _diff_rope_k(h, tabs, half, width, tm, tn):
    t = h.shape[0]
    first = width // tn
    tab_spec = pl.BlockSpec((tm, 128), lambda i, j: (i, 0))
    return pl.pallas_call(
        functools.partial(_diff_rope_kernel, half=half),
        grid=(t // tm, width // tn),
        in_specs=[pl.BlockSpec((tm, tn), lambda i, j: (i, first + j)), tab_spec, tab_spec, tab_spec],
        out_specs=pl.BlockSpec((tm, tn), lambda i, j: (i, j)),
        out_shape=jax.ShapeDtypeStruct((t, width), h.dtype),
        compiler_params=_params("parallel", "arbitrary"),
        name="diff_rope_k",
    )(h, *tabs)


def _diff_attn_kernel(sc_ref, q_ref, c_ref, sn_ref, sp_ref, k_ref, v_ref, g_ref, o_ref, m_ref, l_ref,
                      acc_ref, qs_ref, *, blk, rg, dh, half):
    qi = pl.program_id(2)
    qs_ref[...] = _rope(q_ref[...].astype(F32), c_ref[...], sn_ref[...], sp_ref[...],
                        half).astype(qs_ref.dtype)
    groups = blk // rg
    vrep = 2 * dh // 128

    def block(j, diagonal, first):
        off = pl.multiple_of(j * blk, blk)
        nkeys = [(r + 1) * rg if diagonal else blk for r in range(groups)]
        rows = [slice(r * rg, (r + 1) * rg) for r in range(groups)]
        ss = [[_dot_nt(qs_ref[rows[r], c * dh:(c + 1) * dh],
                       k_ref[pl.ds(off, nkeys[r]), c * dh:(c + 1) * dh])
               for c in range(2)] for r in range(groups)]
        ps, alphas = [], []
        for r, rs in enumerate(rows):
            if diagonal:
                row = lax.broadcasted_iota(jnp.int32, (rg, nkeys[r]), 0) + r * rg
                col = lax.broadcasted_iota(jnp.int32, (rg, nkeys[r]), 1)
                ss[r] = [jnp.where(col <= row, s, -jnp.inf) for s in ss[r]]
            pr, ar = [], []
            for c in range(2):
                top = jnp.max(ss[r][c], -1, keepdims=True)
                if first:
                    m_new = jnp.broadcast_to(top, (rg, 128))
                else:
                    m_old = m_ref[c, rs, :]
                    m_new = jnp.maximum(m_old, top)
                    alpha = jnp.exp2(m_old - m_new)
                p = jnp.exp2(ss[r][c] - jnp.concatenate([m_new] * (nkeys[r] // 128), axis=1))
                psum = jnp.sum(p, -1, keepdims=True)
                if first:
                    l_ref[c, rs, :] = jnp.broadcast_to(psum, (rg, 128))
                else:
                    l_ref[c, rs, :] = alpha * l_ref[c, rs, :] + psum
                    ar.append(jnp.concatenate([alpha] * vrep, axis=1))
                m_ref[c, rs, :] = m_new
                pr.append(p.astype(v_ref.dtype))
            ps.append(pr)
            alphas.append(ar)
        v_all = v_ref[pl.ds(off, blk), :]
        for r, rs in enumerate(rows):
            v = v_all[:nkeys[r]]
            for c in range(2):
                pv = _dot(ps[r][c], v)
                acc_ref[c, rs, :] = pv if first else alphas[r][c] * acc_ref[c, rs, :] + pv

    @pl.when(qi == 0)
    def _():
        block(0, True, True)

    @pl.when(qi > 0)
    def _():
        block(0, False, True)

        def body(j, carry):
            block(j, False, False)
            return carry

        lax.fori_loop(1, qi, body, 0)
        block(qi, True, False)

    lam = sc_ref[0]
    post = sc_ref[1]
    o1 = acc_ref[0] / jnp.concatenate([l_ref[0]] * vrep, axis=1)
    o2 = acc_ref[1] / jnp.concatenate([l_ref[1]] * vrep, axis=1)
    out = o1 - lam * o2
    out = out * lax.rsqrt(jnp.mean(out * out, -1, keepdims=True) + SUBLN_EPS)
    o_ref[...] = (out * g_ref[...] * post).astype(o_ref.dtype)


def _diff_attn(h, k_rot, tabs, half, scalars, subln_g, batch, seq, blk, rg):
    t, width = k_rot.shape
    dh = DIFF_HEAD_DIM
    hw = 2 * dh
    heads = width // hw
    nq = seq // blk
    kern = functools.partial(_diff_attn_kernel, blk=blk, rg=rg, dh=dh, half=half)
    resident = lambda first, bufs: pl.BlockSpec((seq, hw), lambda b, h, qi: (b, first + h),
                                                pipeline_mode=pl.Buffered(bufs))
    tab_spec = pl.BlockSpec((blk, 128), lambda b, h, qi: (b * nq + qi, 0))
    return pl.pallas_call(
        kern,
        grid=(batch, heads, nq),
        in_specs=[pl.BlockSpec(memory_space=pltpu.SMEM),
                  pl.BlockSpec((blk, hw), lambda b, h, qi: (b * nq + qi, h)),
                  tab_spec, tab_spec, tab_spec,
                  resident(0, 2), resident(2 * heads, 1),
                  pl.BlockSpec((1, hw), lambda b, h, qi: (0, 0))],
        out_specs=pl.BlockSpec((blk, hw), lambda b, h, qi: (b * nq + qi, h)),
        out_shape=jax.ShapeDtypeStruct((t, width), BF16),
        scratch_shapes=[pltpu.VMEM((2, blk, 128), F32), pltpu.VMEM((2, blk, 128), F32),
                        pltpu.VMEM((2, blk, hw), F32), pltpu.VMEM((blk, hw), BF16)],
        compiler_params=_params("parallel", "parallel", "arbitrary"),
        name="diff_attn",
    )(scalars, h, *tabs, k_rot, h, subln_g)


def _rwkv_finish(y, g, bvg, gnw, gnb, ones, n):
    ym = _head_sums(y, ones) * (1.0 / n)
    d = y - ym
    yv = _head_sums(d * d, ones) * (1.0 / n)
    return (d * lax.rsqrt(yv + WKV_GN_EPS) * gnw + gnb) * g + bvg


def _even_outproj_ln_kernel(y_ref, g_ref, bvg_ref, gnw_ref, gnb_ref, ones_ref, yb_ref, wa_ref, wb_ref,
                            x_ref, lg_ref, lb_ref, o_ref, obf_ref, *, alpha, n):
    ya = _rwkv_finish(y_ref[...], g_ref[...], bvg_ref[...], gnw_ref[...], gnb_ref[...],
                      ones_ref[...], n)
    mix = _dot(ya.astype(BF16), wa_ref[...]) + _dot(yb_ref[...].astype(BF16), wb_ref[...])
    out = _layer_norm(alpha * x_ref[...] + mix, lg_ref[...], lb_ref[...])
    o_ref[...] = out
    obf_ref[...] = out.astype(BF16)


def _even_outproj_ln(y, g, bvg, gn_w, gn_b, ones, yb, wa, wb, x, lg, lb, alpha, tm):
    t, d = x.shape
    kern = functools.partial(_even_outproj_ln_kernel, alpha=alpha, n=HEAD_DIM)
    row = lambda arr: pl.BlockSpec((tm, arr.shape[1]), lambda i: (i, 0))
    full = lambda arr: pl.BlockSpec(arr.shape, lambda i: (0, 0), pipeline_mode=pl.Buffered(1))
    return pl.pallas_call(
        kern,
        grid=(t // tm,),
        in_specs=[row(y), row(g), row(bvg), full(gn_w), full(gn_b), full(ones), row(yb), full(wa),
                  full(wb), row(x), full(lg), full(lb)],
        out_specs=[row(x), row(x)],
        out_shape=[jax.ShapeDtypeStruct((t, d), F32), jax.ShapeDtypeStruct((t, d), BF16)],
        compiler_params=_params("parallel"),
        name="even_outproj_ln",
    )(y, g, bvg, gn_w, gn_b, ones, yb, wa, wb, x, lg, lb)


def _outproj_ln_kernel(*refs, n_in, alpha):
    ys = refs[:n_in]
    ws = refs[n_in:2 * n_in]
    x_ref, g_ref, b_ref, o_ref, obf_ref = refs[2 * n_in:]
    mix = _dot(ys[0][...].astype(BF16), ws[0][...])
    for y, w in zip(ys[1:], ws[1:]):
        mix = mix + _dot(y[...].astype(BF16), w[...])
    out = _layer_norm(alpha * x_ref[...] + mix, g_ref[...], b_ref[...])
    o_ref[...] = out
    obf_ref[...] = out.astype(BF16)


def _outproj_ln(ys, ws, x, g, b, alpha, tm):
    t, d = x.shape
    n_in = len(ys)
    kern = functools.partial(_outproj_ln_kernel, n_in=n_in, alpha=alpha)
    row = lambda width: pl.BlockSpec((tm, width), lambda i: (i, 0))
    full = lambda arr: pl.BlockSpec(arr.shape, lambda i: (0, 0), pipeline_mode=pl.Buffered(1))
    return pl.pallas_call(
        kern,
        grid=(t // tm,),
        in_specs=[row(y.shape[1]) for y in ys] + [full(w) for w in ws] + [row(d), full(g), full(b)],
        out_specs=[row(d), row(d)],
        out_shape=[jax.ShapeDtypeStruct((t, d), F32), jax.ShapeDtypeStruct((t, d), BF16)],
        compiler_params=_params("parallel"),
        name="outproj_ln",
    )(*ys, *ws, x, g, b)


def _ffn_up_kernel(x_ref, w1_ref, w3_ref, o_ref, w1b_ref, w3b_ref):
    @pl.when(pl.program_id(1) == 0)
    def _():
        w1b_ref[...] = w1_ref[...].astype(BF16)
        w3b_ref[...] = w3_ref[...].astype(BF16)

    x = x_ref[...]
    h1 = _dot(x, w1b_ref[...])
    h3 = _dot(x, w3b_ref[...])
    o_ref[...] = (h1 * jax.nn.sigmoid(h1) * h3).astype(o_ref.dtype)


def _ffn_up(x, w1, w3, layer, tm, tf):
    t, d = x.shape
    f = w1.shape[2]
    return pl.pallas_call(
        _ffn_up_kernel,
        grid=(f // tf, t // tm),
        in_specs=[pl.BlockSpec((tm, d), lambda j, i: (i, 0)),
                  pl.BlockSpec((None, d, tf), lambda j, i: (layer, 0, j)),
                  pl.BlockSpec((None, d, tf), lambda j, i: (layer, 0, j))],
        out_specs=pl.BlockSpec((tm, tf), lambda j, i: (i, j)),
        out_shape=jax.ShapeDtypeStruct((t, f), BF16),
        scratch_shapes=[pltpu.VMEM((d, tf), BF16), pltpu.VMEM((d, tf), BF16)],
        compiler_params=_params("parallel", "arbitrary"),
        name="ffn_up",
    )(x, w1, w3)


def _ffn_down_ple_kernel(a_ref, w2_ref, x_ref, g_ref, b_ref, p_ref, wg_ref, wp_ref, o_ref, obf_ref,
                         *, alpha):
    y = _dot(a_ref[...], w2_ref[...])
    x2 = _layer_norm(alpha * x_ref[...] + y, g_ref[...], b_ref[...])
    gate = jax.nn.sigmoid(_dot(x2.astype(BF16), wg_ref[...]))
    proj = _dot(p_ref[...].astype(BF16), wp_ref[...])
    out = x2 + gate * proj
    o_ref[...] = out
    obf_ref[...] = out.astype(BF16)


def _ffn_down_ple(act, w2, x, g, b, p, layer, wg, wp, alpha, tm):
    t, d = x.shape
    f = act.shape[1]
    pd = p.shape[1]
    p_first = layer * (t // tm)
    kern = functools.partial(_ffn_down_ple_kernel, alpha=alpha)
    row = lambda width: pl.BlockSpec((tm, width), lambda i: (i, 0))
    once = lambda arr: pl.BlockSpec(arr.shape, lambda i: (0, 0), pipeline_mode=pl.Buffered(1))
    return pl.pallas_call(
        kern,
        grid=(t // tm,),
        in_specs=[row(f), once(w2), row(d), once(g), once(b),
                  pl.BlockSpec((tm, pd), lambda i: (p_first + i, 0)), once(wg), once(wp)],
        out_specs=[row(d), row(d)],
        out_shape=[jax.ShapeDtypeStruct((t, d), F32), jax.ShapeDtypeStruct((t, d), BF16)],
        compiler_params=_params("parallel"),
        name="ffn_down_ple",
    )(act, w2, x, g, b, p, wg, wp)


def _even_mixer(xbf, positions, w_in, mu, w0, w_up, a0, a_up, g_up, k_k, k_a, r_k, gn_w, gn_b,
                sinks, batch, seq):
    aw = w0.shape[0]
    dl = w_up.shape[0]
    al = a_up.shape[0]
    gl = g_up.shape[0]
    a_cols = 3 * aw + dl + al + gl
    gl_pad = 384
    assert dl == 64 and al == 64 and gl <= gl_pad
    d = w_in.shape[0]
    pad_cols = jnp.zeros((d, gl_pad - gl), F32)
    w_a = jnp.concatenate([w_in[:, :a_cols], pad_cols], axis=1).astype(BF16)
    w_b = w_in[:, a_cols:].astype(BF16)
    mu_p = jnp.concatenate([mu, jnp.zeros((gl_pad - gl,), F32)])[None, :]
    zeros = jnp.zeros((dl, aw), F32)
    wwa = jnp.concatenate([jnp.concatenate([w_up, zeros], 1), jnp.concatenate([zeros, a_up], 1)], 0)
    gup = jnp.concatenate([g_up, jnp.zeros((gl_pad - gl, aw), F32)], 0)

    h_a = _matmul(xbf, w_a, F32, 2048, 512, "proj_rwkv")
    h_b = _matmul(xbf, w_b, F32, 1024, w_b.shape[1], "proj_swa")

    ones = jnp.kron(jnp.eye(2, dtype=F32), jnp.ones((HEAD_DIM, HEAD_DIM), F32)).astype(BF16)
    at, bt, kt, rt, bh, kh, v, g, bvg, wc = _rwkv_prep(
        h_a, mu_p, w0[None], a0[None], k_k[None], k_a[None], r_k.reshape(1, aw), wwa, gup, ones,
        seq, 256)
    y = _rwkv_scan(at, bt, kt, rt, bh, kh, v, wc, batch, seq)

    tabs, half = _rope_tables(positions, HEAD_DIM, 128)
    yb = _swa(h_b, tabs, half, sinks, batch, seq)
    return (y, g, bvg, gn_w[None], gn_b[None], ones), yb


def _odd_mixer(xbf, positions, w_in, lam_p, subln_g, layer_idx, batch, seq):
    width = w_in.shape[1] // 3
    q_scale = DIFF_HEAD_DIM ** -0.5 * math.log2(math.e)
    w = jnp.concatenate([w_in[:, :width] * q_scale, w_in[:, width:]], axis=1).astype(BF16)
    h = _matmul(xbf, w, BF16, 2048, 1024, "proj_diff")
    tabs, half = _rope_tables(positions, DIFF_HEAD_DIM, 128)
    k_rot = _diff_rope_k(h, tabs, half, width, 1024, 1024)
    lam_init = 0.8 - 0.6 * math.exp(-0.3 * layer_idx)
    lv = lam_p.astype(F32)
    lam = jnp.exp(jnp.sum(lv[0] * lv[1])) - jnp.exp(jnp.sum(lv[2] * lv[3])) + lam_init
    scalars = jnp.stack([lam, jnp.asarray(1.0 - lam_init, F32)]).astype(F32)
    return _diff_attn(h, k_rot, tabs, half, scalars, subln_g[None], batch, seq, 1024, 256)


def kernel(x, p, positions, e_w_in, e_mu, e_w0, e_w_up, e_a0, e_a_up, e_g_up, e_k_k, e_k_a, e_r_k,
           e_gn_w, e_gn_b, e_sinks, e_w_out, o_w_in, o_lambda, o_subln_g, o_w_out,
           ln1_g, ln1_b, ln2_g, ln2_b, ffn_w1, ffn_w3, ffn_w2, ple_w_proj, ple_w_gate):
    batch, seq, d = x.shape
    depth = p.shape[0]
    t = batch * seq
    alpha = (2.0 * depth) ** 0.25
    xf = x.reshape(t, d)
    xbf = xf.astype(BF16)
    p_rows = p.reshape(depth * t, p.shape[-1])
    for i in range(depth):
        j = i // 2
        if i % 2 == 0:
            ya, yb = _even_mixer(xbf, positions, e_w_in[j], e_mu[j], e_w0[j], e_w_up[j], e_a0[j],
                                 e_a_up[j], e_g_up[j], e_k_k[j], e_k_a[j], e_r_k[j], e_gn_w[j],
                                 e_gn_b[j], e_sinks[j], batch, seq)
            aw = ya[0].shape[1]
            w_out = e_w_out[j].astype(BF16)
            xf, xbf = _even_outproj_ln(*ya, yb, w_out[:aw], w_out[aw:], xf, ln1_g[i][None],
                                       ln1_b[i][None], alpha, 512)
        else:
            y = _odd_mixer(xbf, positions, o_w_in[j], o_lambda[j], o_subln_g[j], i, batch, seq)
            xf, xbf = _outproj_ln([y], [o_w_out[j].astype(BF16)], xf, ln1_g[i][None],
                                  ln1_b[i][None], alpha, 512)
        act = _ffn_up(xbf, ffn_w1, ffn_w3, i, 1024, 512)
        xf, xbf = _ffn_down_ple(act, ffn_w2[i].astype(BF16), xf, ln2_g[i][None], ln2_b[i][None],
                                p_rows, i, ple_w_gate[i].astype(BF16),
                                ple_w_proj[i].astype(BF16), alpha, 256)
    return xf.reshape(batch, seq, d)
```
